```python
import numpy as np
import jax, jax.numpy as jnp
from jax import lax

D_MODEL = 1024
BATCH = 8
SEQ = 4096
DEPTH = 1

N_HEADS = 8
N_KV = 2
GQA = N_HEADS // N_KV
HEAD_DIM = 64
ATTN_WIDTH = N_HEADS * HEAD_DIM
KV_COLS = N_KV * HEAD_DIM
N_GATE = 3 * N_HEADS
CMP_LEN = 32
CMP_STRIDE = 16
CMP_HIDDEN = 4 * HEAD_DIM
SEL_BLOCK = 64
SEL_TOPK = 16
WINDOW = 512
Q_BLOCK = 128
FORCE_SCORE = 1.0e4
CONV_GROUPS = 8
CONV_WIDTH = D_MODEL - ATTN_WIDTH
CONV_K = 3
MIX_WIDTH = ATTN_WIDTH + CONV_WIDTH
N_IN = ATTN_WIDTH + 6 * KV_COLS + N_GATE + 3 * CONV_WIDTH
D_FF = 2816
EPS = 1e-6

kernel_name = 'hymba_nsa_shortconv_convffn_adaln'


def rms_norm(x, g):
    xf = x.astype(jnp.float32)
    y = xf * lax.rsqrt(jnp.mean(xf * xf, axis=-1, keepdims=True) + EPS)
    return (y * g.astype(jnp.float32)).astype(x.dtype)


def group_rms_norm(y, g, n_groups):
    shp = y.shape
    yg = y.reshape(shp[:-1] + (n_groups, shp[-1] // n_groups))
    return rms_norm(yg, g.reshape(n_groups, -1)).reshape(shp)


def causal_dwconv(u, w):
    k_taps, s_len = w.shape[0], u.shape[1]
    up = jnp.pad(u, ((0, 0), (k_taps - 1, 0), (0, 0)))
    return sum(up[:, j:j + s_len] * w[j] for j in range(k_taps))


def masked_softmax(s, mask):
    s = jnp.where(mask, s.astype(jnp.float32), -jnp.inf)
    m = jnp.max(s, axis=-1, keepdims=True)
    m = jnp.where(jnp.isfinite(m), m, 0.0)
    e = jnp.where(mask, jnp.exp(s - m), 0.0)
    return e / jnp.maximum(jnp.sum(e, axis=-1, keepdims=True), jnp.finfo(jnp.float32).tiny)


def alibi_slopes(n):
    return np.array([2.0 ** (-8.0 * (h + 1) / n) for h in range(n)], dtype=np.float32)


def compress(k, pos, w1, w2):
    s_len = k.shape[2]
    n_cmp = (s_len - CMP_LEN) // CMP_STRIDE + 1
    idx = np.arange(n_cmp)[:, None] * CMP_STRIDE + np.arange(CMP_LEN)[None, :]
    blocks = k[:, :, idx] + pos
    flat = blocks.reshape(blocks.shape[:3] + (CMP_LEN * HEAD_DIM,))
    return jax.nn.gelu(flat @ w1) @ w2


def nsa_mixer(q, kc, vc, ks, vs, kw, vw, gates):
    b_sz, s_len = q.shape[0], q.shape[1]
    n_cmp = kc.shape[2]
    n_blk = s_len // SEL_BLOCK
    n_sel = min(SEL_TOPK, n_blk)
    scale = HEAD_DIM ** -0.5
    slopes = jnp.asarray(alibi_slopes(N_HEADS).reshape(N_KV, GQA))
    c_end = jnp.arange(n_cmp) * CMP_STRIDE + CMP_LEN - 1
    cs = np.arange(n_cmp) * CMP_STRIDE
    bs = np.arange(n_blk) * SEL_BLOCK
    overlap = jnp.asarray(((cs[None, :] < bs[:, None] + SEL_BLOCK) &
                           (cs[None, :] + CMP_LEN > bs[:, None])).astype(np.float32))
    ks_blk = ks.reshape(b_sz, N_KV, n_blk, SEL_BLOCK, HEAD_DIM)
    vs_blk = vs.reshape(b_sz, N_KV, n_blk, SEL_BLOCK, HEAD_DIM)
    kw_pad = jnp.pad(kw, ((0, 0), (0, 0), (WINDOW, 0), (0, 0)))
    vw_pad = jnp.pad(vw, ((0, 0), (0, 0), (WINDOW, 0), (0, 0)))
    bi = jnp.arange(b_sz)[:, None, None, None]
    gi = jnp.arange(N_KV)[None, :, None, None]
    blk_ids = jnp.arange(n_blk)

    def block(i):
        t0 = i * Q_BLOCK
        tpos = t0 + jnp.arange(Q_BLOCK)
        qb = lax.dynamic_slice_in_dim(q, t0, Q_BLOCK, axis=1)
        gb = lax.dynamic_slice_in_dim(gates, t0, Q_BLOCK, axis=1)
        s = jnp.einsum('btghd,bgnd->bgthn', qb, kc).astype(jnp.float32) * scale
        dist = (tpos[:, None] - c_end[None, :]).astype(jnp.float32)
        s = s - slopes[None, :, None, :, None] * dist[None, None, :, None, :]
        p_c = masked_softmax(s, (c_end[None, :] <= tpos[:, None])[None, None, :, None, :])
        o_c = jnp.einsum('bgthn,bgnd->btghd', p_c.astype(vc.dtype), vc)
        imp = jnp.einsum('bgtn,jn->bgtj', p_c.sum(axis=3), overlap)
        cur = tpos // SEL_BLOCK
        valid = blk_ids[None, :] <= cur[:, None]
        forced = ((blk_ids[None, :] == 0) | (blk_ids[None, :] == cur[:, None]) |
                  (blk_ids[None, :] == cur[:, None] - 1))
        score = jnp.where(valid, imp, -1.0)
        score = jnp.where(forced, FORCE_SCORE, score)
        _, idx = lax.top_k(score, n_sel)
        k_sel = ks_blk[bi, gi, idx]
        v_sel = vs_blk[bi, gi, idx]
        kpos = idx[..., None] * SEL_BLOCK + jnp.arange(SEL_BLOCK)
        s = jnp.einsum('btghd,bgtnkd->bgthnk', qb, k_sel).astype(jnp.float32) * scale
        dist = (tpos[None, None, :, None, None] - kpos).astype(jnp.float32)
        s = s - slopes[None, :, None, :, None, None] * dist[:, :, :, None]
        sh = s.shape
        mask_s = (dist >= 0).reshape(sh[:3] + (-1,))[:, :, :, None, :]
        p_s = masked_softmax(s.reshape(sh[:4] + (-1,)), mask_s).reshape(sh)
        o_s = jnp.einsum('bgthnk,bgtnkd->btghd', p_s.astype(v_sel.dtype), v_sel)
        kwb = lax.dynamic_slice_in_dim(kw_pad, t0, WINDOW + Q_BLOCK, axis=2)
        vwb = lax.dynamic_slice_in_dim(vw_pad, t0, WINDOW + Q_BLOCK, axis=2)
        kpos_w = t0 - WINDOW + jnp.arange(WINDOW + Q_BLOCK)
        dist_w = tpos[:, None] - kpos_w[None, :]
        mask_w = (dist_w >= 0) & (dist_w < WINDOW) & (kpos_w[None, :] >= 0)
        s = jnp.einsum('btghd,bgkd->bgthk', qb, kwb).astype(jnp.float32) * scale
        s = s - slopes[None, :, None, :, None] * dist_w.astype(jnp.float32)[None, None, :, None, :]
        p_w = masked_softmax(s, mask_w[None, None, :, None, :])
        o_w = jnp.einsum('bgthk,bgkd->btghd', p_w.astype(vwb.dtype), vwb)
        return gb[..., 0:1] * o_c + gb[..., 1:2] * o_s + gb[..., 2:3] * o_w

    out = lax.map(block, jnp.arange(s_len // Q_BLOCK))
    return jnp.moveaxis(out, 0, 1).reshape(b_sz, s_len, ATTN_WIDTH)


def setup_inputs(seed: int = 0) -> dict:
    key = jax.random.key(seed)
    k = jax.random.split(key, 32)
    L = DEPTH

    def nrm(kk, shape, std):
        return jax.random.normal(kk, shape, jnp.float32) * std

    return {
        'x': nrm(k[0], (BATCH, SEQ, D_MODEL), 1.0),
        'c': nrm(k[1], (BATCH, D_MODEL), 1.0),
        'w_ada': nrm(k[2], (L, D_MODEL, 6 * D_MODEL), 0.5 * D_MODEL ** -0.5),
        'b_ada': nrm(k[3], (L, 6 * D_MODEL), 0.01),
        'norm1_g': 1.0 + nrm(k[4], (L, D_MODEL), 0.02),
        'w_in': nrm(k[5], (L, D_MODEL, N_IN), D_MODEL ** -0.5),
        'b_gate': nrm(k[6], (L, N_GATE), 0.01),
        'q_norm_g': 1.0 + nrm(k[7], (L, HEAD_DIM), 0.02),
        'k_norm_cmp_g': 1.0 + nrm(k[8], (L, HEAD_DIM), 0.02),
        'k_norm_slc_g': 1.0 + nrm(k[9], (L, HEAD_DIM), 0.02),
        'k_norm_win_g': 1.0 + nrm(k[10], (L, HEAD_DIM), 0.02),
        'pos_cmp_k': nrm(k[11], (L, CMP_LEN, HEAD_DIM), 0.1),
        'pos_cmp_v': nrm(k[12], (L, CMP_LEN, HEAD_DIM), 0.1),
        'w_cmp_k1': nrm(k[13], (L, CMP_LEN * HEAD_DIM, CMP_HIDDEN), (CMP_LEN * HEAD_DIM) ** -0.5),
        'w_cmp_k2': nrm(k[14], (L, CMP_HIDDEN, HEAD_DIM), CMP_HIDDEN ** -0.5),
        'w_cmp_v1': nrm(k[15], (L, CMP_LEN * HEAD_DIM, CMP_HIDDEN), (CMP_LEN * HEAD_DIM) ** -0.5),
        'w_cmp_v2': nrm(k[16], (L, CMP_HIDDEN, HEAD_DIM), CMP_HIDDEN ** -0.5),
        'conv_mix_w': nrm(k[17], (L, CONV_K, CONV_WIDTH), CONV_K ** -0.5),
        'attn_out_g': 1.0 + nrm(k[18], (L, ATTN_WIDTH), 0.02),
        'conv_out_g': 1.0 + nrm(k[19], (L, CONV_WIDTH), 0.02),
        'w_out': nrm(k[20], (L, MIX_WIDTH, D_MODEL), MIX_WIDTH ** -0.5),
        'norm2_g': 1.0 + nrm(k[21], (L, D_MODEL), 0.02),
        'w_ffn_gate': nrm(k[22], (L, D_MODEL, D_FF), D_MODEL ** -0.5),
        'w_ffn_up': nrm(k[23], (L, D_MODEL, D_FF), D_MODEL ** -0.5),
        'conv_ffn_w': nrm(k[24], (L, CONV_K, D_FF), CONV_K ** -0.5),
        'w_ffn_down': nrm(k[25], (L, D_FF, D_MODEL), D_FF ** -0.5),
    }


def reference(x, c, w_ada, b_ada, norm1_g, w_in, b_gate, q_norm_g, k_norm_cmp_g, k_norm_slc_g,
              k_norm_win_g, pos_cmp_k, pos_cmp_v, w_cmp_k1, w_cmp_k2, w_cmp_v1, w_cmp_v2,
              conv_mix_w, attn_out_g, conv_out_g, w_out, norm2_g, w_ffn_gate, w_ffn_up,
              conv_ffn_w, w_ffn_down):
    b_sz, s_len, _ = x.shape
    split_at = list(np.cumsum([ATTN_WIDTH] + [KV_COLS] * 6 + [N_GATE, CONV_WIDTH, CONV_WIDTH]))

    def to_kv(t):
        return t.reshape(b_sz, s_len, N_KV, HEAD_DIM).transpose(0, 2, 1, 3)

    for l in range(DEPTH):
        mod = jax.nn.silu(c) @ w_ada[l] + b_ada[l]
        sh1, sc1, g1, sh2, sc2, g2 = jnp.split(mod[:, None, :], 6, axis=-1)
        h = rms_norm(x, norm1_g[l]) * (1.0 + sc1) + sh1
        proj = h @ w_in[l]
        q, kc_raw, vc_raw, ks, vs, kw, vw, g_logit, gate_b, gate_c, xin = jnp.split(proj, split_at, axis=-1)
        q = rms_norm(q.reshape(b_sz, s_len, N_KV, GQA, HEAD_DIM), q_norm_g[l])
        kc = rms_norm(compress(to_kv(kc_raw), pos_cmp_k[l], w_cmp_k1[l], w_cmp_k2[l]), k_norm_cmp_g[l])
        vc = compress(to_kv(vc_raw), pos_cmp_v[l], w_cmp_v1[l], w_cmp_v2[l])
        ks = rms_norm(to_kv(ks), k_norm_slc_g[l])
        kw = rms_norm(to_kv(kw), k_norm_win_g[l])
        gates = jax.nn.sigmoid(g_logit + b_gate[l]).reshape(b_sz, s_len, N_KV, GQA, 3)
        attn = nsa_mixer(q, kc, vc, ks, to_kv(vs), kw, to_kv(vw), gates)
        conv = gate_b * causal_dwconv(gate_c * xin, conv_mix_w[l])
        mixed = jnp.concatenate([group_rms_norm(attn, attn_out_g[l], N_HEADS),
                                 group_rms_norm(conv, conv_out_g[l], CONV_GROUPS)], axis=-1)
        x = x + g1 * (mixed @ w_out[l])
        h2 = rms_norm(x, norm2_g[l]) * (1.0 + sc2) + sh2
        g_pre = causal_dwconv(h2 @ w_ffn_gate[l], conv_ffn_w[l])
        x = x + g2 * ((jax.nn.silu(g_pre) * (h2 @ w_ffn_up[l])) @ w_ffn_down[l])
    return x
```

```python
import functools

import numpy as np
import jax
import jax.numpy as jnp
from jax import lax
from jax.experimental import pallas as pl
from jax.experimental.pallas import tpu as pltpu

F32 = jnp.float32
BF16 = jnp.bfloat16

N_HEADS = 8
N_KV = 2
GQA = N_HEADS // N_KV
HEAD_DIM = 64
ATTN_WIDTH = N_HEADS * HEAD_DIM
KV_COLS = N_KV * HEAD_DIM
N_GATE = 3 * N_HEADS
CMP_LEN = 32
CMP_STRIDE = 16
SEL_BLOCK = 64
SEL_TOPK = 16
WINDOW = 512
FORCE_SCORE = 1.0e4
CONV_WIDTH = 512
EPS = 1e-6
NEG = -1.0e30
TINY = float(np.finfo(np.float32).tiny)

LANES = 128
GROUPS_PER_MXU_TILE = 4
MXU_TILE = GROUPS_PER_MXU_TILE * HEAD_DIM
VMEM_LIMIT = 56 * 1024 * 1024

TM_PROJ = 512
TM_FFN = 512
FF_CHUNK = 256
Q_TILE = 128
K_CHUNK = 256
HALO = 16


def _dot(a, b):
    return jnp.dot(a, b, preferred_element_type=F32)


def _dot_nt(a, b):
    return lax.dot_general(a, b, (((1,), (1,)), ((), ())), preferred_element_type=F32)


def _group_rms(v, gmat, gain):
    ssq = _dot((v * v).astype(BF16), gmat)
    return v * lax.rsqrt(ssq * (1.0 / HEAD_DIM) + EPS) * gain


def _ada_kernel(c_ref, w_ref, b_ref, o_ref):
    a = jax.nn.silu(c_ref[...])
    o_ref[...] = jnp.dot(a, w_ref[...], preferred_element_type=F32,
                         precision=lax.Precision.HIGHEST) + b_ref[...]


def _ada(c, w, b):
    bsz, d = c.shape
    n = w.shape[1]
    return pl.pallas_call(
        _ada_kernel,
        out_shape=jax.ShapeDtypeStruct((bsz, n), F32),
        grid=(n // d,),
        in_specs=[pl.BlockSpec((bsz, d), lambda j: (0, 0)),
                  pl.BlockSpec((d, d), lambda j: (0, j)),
                  pl.BlockSpec((1, d), lambda j: (0, j))],
        out_specs=pl.BlockSpec((bsz, d), lambda j: (0, j)),
        compiler_params=pltpu.CompilerParams(dimension_semantics=("arbitrary",),
                                             vmem_limit_bytes=VMEM_LIMIT),
        name="ada",
    )(c, w, b)


def _inproj_kernel(x_ref, mod_ref, n1_ref, wq_ref, wkv_ref, wgl_ref, wc_ref, bg_ref, qg_ref, kg_ref,
                   cw_ref, cg_ref, gmat_ref,
                   q_out, kvc_out, ks_out, vs_out, kw_out, vw_out, gate_out, conv_out, u_scr):
    i = pl.program_id(1)
    tm = x_ref.shape[1]
    x = x_ref[0]
    mod = mod_ref[0]
    ms = jnp.mean(x * x, axis=-1, keepdims=True)
    y = x * lax.rsqrt(ms + EPS) * n1_ref[...]
    h = y * (1.0 + mod[1:2]) + mod[0:1]
    hb = h.astype(BF16)
    gmat = gmat_ref[...]

    pq = _dot(hb, wq_ref[...])
    for j in range(ATTN_WIDTH // MXU_TILE):
        sl = slice(MXU_TILE * j, MXU_TILE * (j + 1))
        qn = _group_rms(pq[:, sl], gmat, qg_ref[...]) * (HEAD_DIM ** -0.5)
        q_out[0, :, sl] = qn.astype(BF16)

    pkv = _dot(hb, wkv_ref[...])
    kvc_out[0] = pkv[:, 0:2 * KV_COLS]
    ksw = jnp.concatenate([pkv[:, 2 * KV_COLS:3 * KV_COLS], pkv[:, 4 * KV_COLS:5 * KV_COLS]], axis=-1)
    kswn = _group_rms(ksw, gmat, kg_ref[...]).astype(BF16)
    vs = pkv[:, 3 * KV_COLS:4 * KV_COLS].astype(BF16)
    vw = pkv[:, 5 * KV_COLS:6 * KV_COLS].astype(BF16)
    for g in range(N_KV):
        ks_out[0, g] = kswn[:, HEAD_DIM * g:HEAD_DIM * (g + 1)]
        kw_out[0, g] = kswn[:, KV_COLS + HEAD_DIM * g:KV_COLS + HEAD_DIM * (g + 1)]
        vs_out[0, g] = vs[:, HEAD_DIM * g:HEAD_DIM * (g + 1)]
        vw_out[0, g] = vw[:, HEAD_DIM * g:HEAD_DIM * (g + 1)]

    gate_out[0] = jax.nn.sigmoid(_dot(hb, wgl_ref[...]) + bg_ref[...])

    pc = _dot(hb, wc_ref[...])
    gate_b = pc[:, 0:CONV_WIDTH]
    u = pc[:, CONV_WIDTH:2 * CONV_WIDTH] * pc[:, 2 * CONV_WIDTH:3 * CONV_WIDTH]

    @pl.when(i == 0)
    def _():
        u_scr[0:HALO, :] = jnp.zeros((HALO, CONV_WIDTH), F32)

    u_scr[HALO:HALO + tm, :] = u
    cw = cw_ref[...]
    conv = gate_b * (cw[0:1] * u_scr[pl.ds(HALO - 2, tm), :] + cw[1:2] * u_scr[pl.ds(HALO - 1, tm), :]
                     + cw[2:3] * u)
    u_scr[0:HALO, :] = u_scr[tm:tm + HALO, :]
    for j in range(CONV_WIDTH // MXU_TILE):
        sl = slice(MXU_TILE * j, MXU_TILE * (j + 1))
        conv_out[0, :, sl] = _group_rms(conv[:, sl], gmat, cg_ref[:, sl]).astype(BF16)


def _in_proj(x, mod3, n1, wq, wkv, wgl, wc, bg, qg, kg, cw, cg, gmat):
    bsz, s, d = x.shape
    tm = TM_PROJ
    const = lambda shape: pl.BlockSpec(shape, lambda b, i: (0,) * len(shape))
    tok = lambda w: pl.BlockSpec((1, tm, w), lambda b, i: (b, i, 0))
    kvspec = pl.BlockSpec((1, N_KV, tm, HEAD_DIM), lambda b, i: (b, 0, i, 0))
    kv_shape = jax.ShapeDtypeStruct((bsz, N_KV, s, HEAD_DIM), BF16)
    return pl.pallas_call(
        _inproj_kernel,
        out_shape=(jax.ShapeDtypeStruct((bsz, s, ATTN_WIDTH), BF16),
                   jax.ShapeDtypeStruct((bsz, s, 2 * KV_COLS), F32),
                   kv_shape, kv_shape, kv_shape, kv_shape,
                   jax.ShapeDtypeStruct((bsz, s, LANES), F32),
                   jax.ShapeDtypeStruct((bsz, s, CONV_WIDTH), BF16)),
        grid=(bsz, s // tm),
        in_specs=[tok(d),
                  pl.BlockSpec((1, 6, d), lambda b, i: (b, 0, 0)),
                  const(n1.shape), const(wq.shape), const(wkv.shape), const(wgl.shape), const(wc.shape),
                  const(bg.shape), const(qg.shape), const(kg.shape), const(cw.shape), const(cg.shape),
                  const(gmat.shape)],
        out_specs=(tok(ATTN_WIDTH), tok(2 * KV_COLS), kvspec, kvspec, kvspec, kvspec, tok(LANES),
                   tok(CONV_WIDTH)),
        scratch_shapes=[pltpu.VMEM((tm + HALO, CONV_WIDTH), F32)],
        compiler_params=pltpu.CompilerParams(dimension_semantics=("arbitrary", "arbitrary"),
                                             vmem_limit_bytes=VMEM_LIMIT),
        name="in_proj",
    )(x, mod3, n1, wq, wkv, wgl, wc, bg, qg, kg, cw, cg, gmat)


def _cmp_kernel(k_ref, pos_ref, w1_ref, w2_ref, g_ref, o_ref):
    ncp = k_ref.shape[2]
    a = k_ref[0, 0]
    pos = pos_ref[0]
    ha = _dot((a + pos[0:1]).astype(BF16), w1_ref[0, 0])
    hb = _dot((a + pos[1:2]).astype(BF16), w1_ref[0, 1])
    h = ha + pltpu.roll(hb, ncp - 1, axis=0)
    c = _dot(jax.nn.gelu(h).astype(BF16), w2_ref[0])
    normed = c * lax.rsqrt(jnp.mean(c * c, axis=-1, keepdims=True) + EPS) * g_ref[...]
    o_ref[0, 0] = jnp.where(pl.program_id(0) == 0, normed, c).astype(BF16)


def _compress(k16, pos2, w1, w2, g):
    _, nbk, ncp, w = k16.shape
    return pl.pallas_call(
        _cmp_kernel,
        out_shape=jax.ShapeDtypeStruct((2, nbk, ncp, HEAD_DIM), BF16),
        grid=(2, nbk),
        in_specs=[pl.BlockSpec((1, 1, ncp, w), lambda kv, j: (kv, j, 0, 0)),
                  pl.BlockSpec((1, 2, w), lambda kv, j: (kv, 0, 0)),
                  pl.BlockSpec((1, 2, w, w1.shape[-1]), lambda kv, j: (kv, 0, 0, 0)),
                  pl.BlockSpec((1, w2.shape[1], HEAD_DIM), lambda kv, j: (kv, 0, 0)),
                  pl.BlockSpec((1, HEAD_DIM), lambda kv, j: (0, 0))],
        out_specs=pl.BlockSpec((1, 1, ncp, HEAD_DIM), lambda kv, j: (kv, j, 0, 0)),
        compiler_params=pltpu.CompilerParams(dimension_semantics=("arbitrary", "arbitrary"),
                                             vmem_limit_bytes=VMEM_LIMIT),
        name="compress",
    )(k16, pos2, w1, w2, g)


def _attn_kernel(q_ref, gate_ref, kc_ref, vc_ref, ks_ref, vs_ref, kw_ref, vw_ref, e_ref, ovl_ref, og_ref,
                 o_ref, score_scr):
    T = q_ref.shape[1]
    KC = e_ref.shape[2]
    NB = ovl_ref.shape[0]
    NCP = ovl_ref.shape[1]
    nsel = min(SEL_TOPK, NB)
    i = pl.program_id(1)
    t0 = i * T

    rel = (lax.broadcasted_iota(jnp.int32, (T, KC), 0) - lax.broadcasted_iota(jnp.int32, (T, KC), 1))
    gates = gate_ref[0]
    og = og_ref[...]
    c_hi = (t0 + T - 1) // KC + 1
    c_lo_win = jnp.maximum(t0 - WINDOW + 1, 0) // KC

    tpos_c = t0 + lax.broadcasted_iota(jnp.int32, (T, NCP), 0)
    cend = CMP_STRIDE * lax.broadcasted_iota(jnp.int32, (T, NCP), 1) + (CMP_LEN - 1)
    dc = tpos_c - cend
    mask_c = dc >= 0
    dcf = dc.astype(F32)

    blk = lax.broadcasted_iota(jnp.int32, (NB, T), 0)
    cur = (t0 + lax.broadcasted_iota(jnp.int32, (NB, T), 1)) // SEL_BLOCK
    valid = blk <= cur
    forced = (blk == 0) | (blk == cur) | (blk == cur - 1)
    row8 = lax.broadcasted_iota(jnp.int32, (8, T), 0)

    pieces = []
    for g in range(N_KV):
        qg = q_ref[0, :, GQA * HEAD_DIM * g:GQA * HEAD_DIM * (g + 1)]
        q4 = jnp.concatenate([qg[:, HEAD_DIM * h:HEAD_DIM * (h + 1)] for h in range(GQA)], axis=0)
        slopes = [2.0 ** (-8.0 * (GQA * g + h + 1) / N_HEADS) for h in range(GQA)]

        s = _dot_nt(q4, kc_ref[0, g])
        ps = []
        for h in range(GQA):
            sh = jnp.where(mask_c, s[h * T:(h + 1) * T] - slopes[h] * dcf, NEG)
            m = jnp.max(sh, axis=-1, keepdims=True)
            m = jnp.where(m > 0.5 * NEG, m, 0.0)
            e = jnp.where(mask_c, jnp.exp(sh - m), 0.0)
            ps.append(e / jnp.maximum(jnp.sum(e, axis=-1, keepdims=True), TINY))
        oc4 = _dot(jnp.concatenate(ps, axis=0).astype(BF16), vc_ref[0, g])

        psum = ((ps[0] + ps[1]) + ps[2]) + ps[3]
        p_hi = psum.astype(BF16)
        p_lo = (psum - p_hi.astype(F32)).astype(BF16)
        ovl = ovl_ref[...]
        imp_t = _dot_nt(ovl, p_hi) + _dot_nt(ovl, p_lo)
        score = jnp.where(forced, FORCE_SCORE, jnp.where(valid, imp_t, -1.0))
        score_scr[...] = score
        n_grp = NB // 8
        grp = [score[8 * r:8 * (r + 1)] for r in range(n_grp)]
        rank = [jnp.zeros((8, T), jnp.int32) for _ in range(n_grp)]
        for ii in range(NB):
            si = score_scr[pl.ds(ii, 1), :]
            r0 = ii // 8
            for r in range(n_grp):
                ge = jnp.where(si >= grp[r], 1, 0)
                gt = jnp.where(si > grp[r], 1, 0)
                if r > r0:
                    beats = ge
                elif r < r0:
                    beats = gt
                else:
                    beats = jnp.where(row8 + 8 * r > ii, ge, gt)
                rank[r] = rank[r] + beats
        sel = (jnp.concatenate(rank, axis=0) < nsel) & valid
        selb = jnp.where(sel, 0.0, NEG)
        if NB < LANES:
            selb = jnp.concatenate([selb, jnp.full((LANES - NB, T), NEG, F32)], axis=0)
        selb_t = selb.T.astype(BF16)

        def run_branch(k_ref, v_ref, c_lo, bias_fn):
            def body(c, carry):
                m4, l4, acc4 = carry
                k0 = pl.multiple_of(c * KC, KC)
                k = k_ref[0, g, pl.ds(k0, KC), :]
                v = v_ref[0, g, pl.ds(k0, KC), :]
                dist = rel + (t0 - k0)
                distf = dist.astype(F32)
                bias = bias_fn(c, dist)
                adj = jnp.concatenate([bias - slopes[h] * distf for h in range(GQA)], axis=0)
                sc = _dot_nt(q4, k) + adj
                m_new = jnp.maximum(m4, jnp.max(sc, axis=-1, keepdims=True))
                p = jnp.exp(sc - m_new)
                alpha = jnp.exp(m4 - m_new)
                l_new = alpha * l4 + jnp.sum(p, axis=-1, keepdims=True)
                acc_new = alpha * acc4 + _dot(p.astype(BF16), v)
                return m_new, l_new, acc_new

            init = (jnp.full((GQA * T, 1), 2.0 * NEG, F32), jnp.zeros((GQA * T, 1), F32),
                    jnp.zeros((GQA * T, HEAD_DIM), F32))
            _, l4, acc4 = lax.fori_loop(c_lo, c_hi, body, init)
            return acc4 / jnp.maximum(l4, TINY)

        def bias_sel(c, dist):
            return jnp.where(dist >= 0, _dot(selb_t, e_ref[c]), NEG)

        def bias_win(c, dist):
            return jnp.where((dist >= 0) & (dist < WINDOW), 0.0, NEG)

        os4 = run_branch(ks_ref, vs_ref, 0, bias_sel)
        ow4 = run_branch(kw_ref, vw_ref, c_lo_win, bias_win)

        for h in range(GQA):
            hh = GQA * g + h
            sl = slice(h * T, (h + 1) * T)
            o = (gates[:, hh:hh + 1] * oc4[sl] + gates[:, N_HEADS + hh:N_HEADS + hh + 1] * os4[sl]
                 + gates[:, 2 * N_HEADS + hh:2 * N_HEADS + hh + 1] * ow4[sl])
            msq = jnp.mean(o * o, axis=-1, keepdims=True)
            pieces.append(o * lax.rsqrt(msq + EPS) * og[:, HEAD_DIM * hh:HEAD_DIM * (hh + 1)])

    for j in range(N_HEADS // 2):
        o_ref[0, :, LANES * j:LANES * (j + 1)] = jnp.concatenate(
            [pieces[2 * j], pieces[2 * j + 1]], axis=-1).astype(BF16)


def _attention(q, gates, kc, vc, ks, vs, kw, vw, emat, ovl, og):
    bsz, s, _ = q.shape
    T = Q_TILE
    ncp = kc.shape[2]
    nb = ovl.shape[0]
    kvfull = pl.BlockSpec((1, N_KV, s, HEAD_DIM), lambda b, i: (b, 0, 0, 0))
    cmpfull = pl.BlockSpec((1, N_KV, ncp, HEAD_DIM), lambda b, i: (b, 0, 0, 0))
    return pl.pallas_call(
        _attn_kernel,
        out_shape=jax.ShapeDtypeStruct((bsz, s, ATTN_WIDTH), BF16),
        grid=(bsz, s // T),
        in_specs=[pl.BlockSpec((1, T, ATTN_WIDTH), lambda b, i: (b, i, 0)),
                  pl.BlockSpec((1, T, LANES), lambda b, i: (b, i, 0)),
                  cmpfull, cmpfull, kvfull, kvfull, kvfull, kvfull,
                  pl.BlockSpec(emat.shape, lambda b, i: (0, 0, 0)),
                  pl.BlockSpec(ovl.shape, lambda b, i: (0, 0)),
                  pl.BlockSpec(og.shape, lambda b, i: (0, 0))],
        out_specs=pl.BlockSpec((1, T, ATTN_WIDTH), lambda b, i: (b, i, 0)),
        scratch_shapes=[pltpu.VMEM((nb, T), F32)],
        compiler_params=pltpu.CompilerParams(dimension_semantics=("arbitrary", "arbitrary"),
                                             vmem_limit_bytes=VMEM_LIMIT),
        name="nsa_attn",
    )(q, gates, kc, vc, ks, vs, kw, vw, emat, ovl, og)


def _ffn_kernel(x_ref, a_ref, cv_ref, mod_ref, n2_ref, wo_ref, wg_ref, wu_ref, wd_ref, cw_ref,
                o_ref, h2_scr, acc_scr, gp_scr):
    i = pl.program_id(1)
    tm = x_ref.shape[1]
    n_chunks = wg_ref.shape[0]
    mod = mod_ref[0]
    mix = _dot(a_ref[0], wo_ref[0:ATTN_WIDTH, :]) + _dot(cv_ref[0], wo_ref[ATTN_WIDTH:ATTN_WIDTH + CONV_WIDTH, :])
    x1 = x_ref[0] + mod[2:3] * mix
    o_ref[0] = x1
    ms = jnp.mean(x1 * x1, axis=-1, keepdims=True)
    h2 = (x1 * lax.rsqrt(ms + EPS) * n2_ref[...]) * (1.0 + mod[4:5]) + mod[3:4]

    @pl.when(i == 0)
    def _():
        h2_scr[0:HALO, :] = jnp.zeros((HALO, h2_scr.shape[1]), BF16)

    h2_scr[HALO:HALO + tm, :] = h2.astype(BF16)
    acc_scr[...] = jnp.zeros(acc_scr.shape, F32)

    def body(c, carry):
        gp_scr[...] = _dot(h2_scr[...], wg_ref[c])
        cw = cw_ref[c]
        g_pre = (cw[0:1] * gp_scr[pl.ds(HALO - 2, tm), :] + cw[1:2] * gp_scr[pl.ds(HALO - 1, tm), :]
                 + cw[2:3] * gp_scr[pl.ds(HALO, tm), :])
        up = _dot(h2_scr[pl.ds(HALO, tm), :], wu_ref[c])
        act = (jax.nn.silu(g_pre) * up).astype(BF16)
        acc_scr[...] += _dot(act, wd_ref[c])
        return carry

    lax.fori_loop(0, n_chunks, body, 0)
    o_ref[0] = o_ref[0] + mod[5:6] * acc_scr[...]
    h2_scr[0:HALO, :] = h2_scr[tm:tm + HALO, :]


def _out_ffn(x, attn_n, conv_n, mod3, n2, wo, wg3, wu3, wd3, cw3):
    bsz, s, d = x.shape
    tm = TM_FFN
    resident = lambda shape: pl.BlockSpec(shape, lambda b, i: (0,) * len(shape), pipeline_mode=pl.Buffered(1))
    tok = lambda w: pl.BlockSpec((1, tm, w), lambda b, i: (b, i, 0))
    return pl.pallas_call(
        _ffn_kernel,
        out_shape=jax.ShapeDtypeStruct((bsz, s, d), F32),
        grid=(bsz, s // tm),
        in_specs=[tok(d), tok(ATTN_WIDTH), tok(CONV_WIDTH),
                  pl.BlockSpec((1, 6, d), lambda b, i: (b, 0, 0)),
                  resident(n2.shape), resident(wo.shape), resident(wg3.shape), resident(wu3.shape),
                  resident(wd3.shape), resident(cw3.shape)],
        out_specs=tok(d),
        scratch_shapes=[pltpu.VMEM((tm + HALO, d), BF16),
                        pltpu.VMEM((tm, d), F32),
                        pltpu.VMEM((tm + HALO, FF_CHUNK), F32)],
        compiler_params=pltpu.CompilerParams(dimension_semantics=("arbitrary", "arbitrary"),
                                             vmem_limit_bytes=VMEM_LIMIT),
        name="out_ffn",
    )(x, attn_n, conv_n, mod3, n2, wo, wg3, wu3, wd3, cw3)


def _static_tables(s):
    ncp = s // CMP_STRIDE
    n_cmp = (s - CMP_LEN) // CMP_STRIDE + 1
    nb = s // SEL_BLOCK
    cs = np.arange(ncp) * CMP_STRIDE
    bs = np.arange(nb) * SEL_BLOCK
    ovl = ((cs[None, :] < bs[:, None] + SEL_BLOCK) & (cs[None, :] + CMP_LEN > bs[:, None])
           & (np.arange(ncp)[None, :] < n_cmp)).astype(np.float32)
    kpos = np.arange(s)
    nbp = max(nb, LANES)
    emat = (np.arange(nbp)[:, None] == (kpos[None, :] // SEL_BLOCK)).astype(np.float32)
    emat = emat.reshape(nbp, s // K_CHUNK, K_CHUNK).transpose(1, 0, 2)
    gmat = (np.arange(MXU_TILE)[:, None] // HEAD_DIM == np.arange(MXU_TILE)[None, :] // HEAD_DIM)
    return (jnp.asarray(ovl, BF16), jnp.asarray(emat, BF16), jnp.asarray(gmat.astype(np.float32), BF16))


def kernel(x, c, w_ada, b_ada, norm1_g, w_in, b_gate, q_norm_g, k_norm_cmp_g, k_norm_slc_g, k_norm_win_g,
           pos_cmp_k, pos_cmp_v, w_cmp_k1, w_cmp_k2, w_cmp_v1, w_cmp_v2, conv_mix_w, attn_out_g, conv_out_g,
           w_out, norm2_g, w_ffn_gate, w_ffn_up, conv_ffn_w, w_ffn_down):
    bsz, s, d = x.shape
    assert w_ada.shape[0] == 1, "single layer"
    assert s % TM_PROJ == 0 and s % TM_FFN == 0 and s % K_CHUNK == 0 and s % Q_TILE == 0
    d_ff = w_ffn_gate.shape[-1]
    assert d_ff % FF_CHUNK == 0
    n_ch = d_ff // FF_CHUNK
    ovl, emat, gmat = _static_tables(s)

    mod = _ada(c, w_ada[0], b_ada[0][None, :])
    mod3 = mod.reshape(bsz, 6, d)

    w = w_in[0]
    o_q, o_kv, o_gl = ATTN_WIDTH, ATTN_WIDTH + 6 * KV_COLS, ATTN_WIDTH + 6 * KV_COLS + N_GATE
    wq = w[:, :o_q].astype(BF16)
    wkv = w[:, o_q:o_kv].astype(BF16)
    perm = np.array([kv * GQA * 3 + h * 3 + br for br in range(3) for kv in range(N_KV) for h in range(GQA)])
    wgl = jnp.pad(w[:, o_kv:o_gl][:, perm], ((0, 0), (0, LANES - N_GATE))).astype(BF16)
    bg = jnp.pad(b_gate[0][perm], (0, LANES - N_GATE))[None, :]
    wc = w[:, o_gl:].astype(BF16)
    qg = jnp.tile(q_norm_g[0], GROUPS_PER_MXU_TILE)[None, :]
    kg = jnp.concatenate([jnp.tile(k_norm_slc_g[0], N_KV), jnp.tile(k_norm_win_g[0], N_KV)])[None, :]

    q, kvc_raw, ks, vs, kw, vw, gates, conv_n = _in_proj(
        x, mod3, norm1_g, wq, wkv, wgl, wc, bg, qg, kg, conv_mix_w[0], conv_out_g, gmat)

    ncp = s // CMP_STRIDE
    k16 = kvc_raw.reshape(bsz, s, 2, N_KV, HEAD_DIM).transpose(2, 0, 3, 1, 4)
    k16 = k16.reshape(2, bsz * N_KV, ncp, CMP_STRIDE * HEAD_DIM)
    half = CMP_STRIDE * HEAD_DIM
    pos2 = jnp.stack([pos_cmp_k[0].reshape(2, half), pos_cmp_v[0].reshape(2, half)])
    w1 = jnp.stack([w_cmp_k1[0], w_cmp_v1[0]]).reshape(2, 2, half, -1).astype(BF16)
    w2 = jnp.stack([w_cmp_k2[0], w_cmp_v2[0]]).astype(BF16)
    cmp = _compress(k16, pos2, w1, w2, k_norm_cmp_g)
    cmp = cmp.reshape(2, bsz, N_KV, ncp, HEAD_DIM)

    attn_n = _attention(q, gates, cmp[0], cmp[1], ks, vs, kw, vw, emat, ovl, attn_out_g)

    wo = w_out[0].astype(BF16)
    wg3 = w_ffn_gate[0].reshape(d, n_ch, FF_CHUNK).transpose(1, 0, 2).astype(BF16)
    wu3 = w_ffn_up[0].reshape(d, n_ch, FF_CHUNK).transpose(1, 0, 2).astype(BF16)
    wd3 = w_ffn_down[0].reshape(n_ch, FF_CHUNK, d).astype(BF16)
    cw3 = jnp.pad(conv_ffn_w[0], ((0, 8 - conv_ffn_w.shape[1]), (0, 0)))
    cw3 = cw3.reshape(8, n_ch, FF_CHUNK).transpose(1, 0, 2)
    return _out_ffn(x, attn_n, conv_n, mod3, norm2_g, wo, wg3, wu3, wd3, cw3)
```

```python
import numpy as np
import jax
import jax.numpy as jnp
from jax import lax
from jax.experimental import pallas as pl
from jax.experimental.pallas import tpu as pltpu

F32 = jnp.float32
BF16 = jnp.bfloat16

N_HEADS = 8
N_KV = 2
GQA = N_HEADS // N_KV
HEAD_DIM = 64
ATTN_WIDTH = N_HEADS * HEAD_DIM
KV_COLS = N_KV * HEAD_DIM
N_GATE = 3 * N_HEADS
CMP_LEN = 32
CMP_STRIDE = 16
SEL_BLOCK = 64
SEL_TOPK = 16
WINDOW = 512
FORCE_SCORE = 1.0e4
CONV_WIDTH = 512
EPS = 1e-6
NEG = -1.0e30
TINY = float(np.finfo(np.float32).tiny)

LANES = 128
BF16_SUBLANES = 16
GROUPS_PER_MXU_TILE = 4
MXU_TILE = GROUPS_PER_MXU_TILE * HEAD_DIM
VMEM_LIMIT = 56 * 1024 * 1024

TM_PROJ = 512
TM_FFN = 512
FF_CHUNK = 256
Q_TILE = 128
K_CHUNK = 256
HALO = 16

POS_FEATS = 4
GATE_ROWS = 32
V_ROWS = HEAD_DIM + BF16_SUBLANES
K_SEL_WIDTH = 2 * LANES


def _dot(a, b):
    return jnp.dot(a, b, preferred_element_type=F32)


def _group_rms(v, gmat, gain):
    ssq = _dot((v * v).astype(BF16), gmat)
    return v * lax.rsqrt(ssq * (1.0 / HEAD_DIM) + EPS) * gain


def _ada_kernel(c_ref, w_ref, b_ref, o_ref):
    a = jax.nn.silu(c_ref[...])
    o_ref[...] = jnp.dot(a, w_ref[...], preferred_element_type=F32,
                         precision=lax.Precision.HIGHEST) + b_ref[...]


def _ada(c, w, b):
    bsz, d = c.shape
    n = w.shape[1]
    return pl.pallas_call(
        _ada_kernel,
        out_shape=jax.ShapeDtypeStruct((bsz, n), F32),
        grid=(n // d,),
        in_specs=[pl.BlockSpec((bsz, d), lambda j: (0, 0)),
                  pl.BlockSpec((d, d), lambda j: (0, j)),
                  pl.BlockSpec((1, d), lambda j: (0, j))],
        out_specs=pl.BlockSpec((bsz, d), lambda j: (0, j)),
        compiler_params=pltpu.CompilerParams(dimension_semantics=("arbitrary",),
                                             vmem_limit_bytes=VMEM_LIMIT),
        name="ada",
    )(c, w, b)


def _inproj_kernel(x_ref, mod_ref, n1_ref, wq_ref, wkv_ref, wgl_ref, wc_ref, bg_ref, qg_ref, kg_ref,
                   cw_ref, cg_ref, gmat_ref, qfeat_ref, ktab_ref,
                   qt_out, kvc_out, ks_out, vst_out, kw_out, vwt_out, gatet_out, conv_out, u_scr):
    i = pl.program_id(1)
    tm = x_ref.shape[1]
    x = x_ref[0]
    mod = mod_ref[0]
    ms = jnp.mean(x * x, axis=-1, keepdims=True)
    y = x * lax.rsqrt(ms + EPS) * n1_ref[...]
    h = y * (1.0 + mod[1:2]) + mod[0:1]
    hb = h.astype(BF16)
    gmat = gmat_ref[...]

    pq = _dot(hb, wq_ref[...])
    qfeat = qfeat_ref[...].astype(F32)
    for j in range(ATTN_WIDTH // MXU_TILE):
        qn = _group_rms(pq[:, MXU_TILE * j:MXU_TILE * (j + 1)], gmat, qg_ref[...]) * (HEAD_DIM ** -0.5)
        for hl in range(GROUPS_PER_MXU_TILE):
            hh = GROUPS_PER_MXU_TILE * j + hl
            qa = jnp.concatenate([qn[:, HEAD_DIM * hl:HEAD_DIM * (hl + 1)],
                                  qfeat[:, HEAD_DIM * hh:HEAD_DIM * (hh + 1)]], axis=-1)
            qt_out[0, hh] = qa.T.astype(BF16)

    pkv = _dot(hb, wkv_ref[...])
    kvc_out[0] = pkv[:, 0:2 * KV_COLS]
    ksw = jnp.concatenate([pkv[:, 2 * KV_COLS:3 * KV_COLS], pkv[:, 4 * KV_COLS:5 * KV_COLS]], axis=-1)
    kswn = _group_rms(ksw, gmat, kg_ref[...])
    ktab = ktab_ref[...].astype(F32)
    vs_t = pkv[:, 3 * KV_COLS:4 * KV_COLS].T.astype(BF16)
    vw_t = pkv[:, 5 * KV_COLS:6 * KV_COLS].T.astype(BF16)
    kc_n = K_CHUNK
    ones_rows = jnp.where(lax.broadcasted_iota(jnp.int32, (BF16_SUBLANES, kc_n), 0) == 0, 1.0, 0.0).astype(BF16)
    for g in range(N_KV):
        ks_out[0, g] = jnp.concatenate([kswn[:, HEAD_DIM * g:HEAD_DIM * (g + 1)], ktab], axis=-1).astype(BF16)
        kw_out[0, g] = jnp.concatenate([kswn[:, KV_COLS + HEAD_DIM * g:KV_COLS + HEAD_DIM * (g + 1)],
                                        ktab[:, 0:HEAD_DIM]], axis=-1).astype(BF16)
        for j in range(tm // kc_n):
            cols = slice(kc_n * j, kc_n * (j + 1))
            vst_out[0, g, j, 0:HEAD_DIM, :] = vs_t[HEAD_DIM * g:HEAD_DIM * (g + 1), cols]
            vst_out[0, g, j, HEAD_DIM:V_ROWS, :] = ones_rows
            vwt_out[0, g, j, 0:HEAD_DIM, :] = vw_t[HEAD_DIM * g:HEAD_DIM * (g + 1), cols]
            vwt_out[0, g, j, HEAD_DIM:V_ROWS, :] = ones_rows

    gate = jax.nn.sigmoid(_dot(hb, wgl_ref[...]) + bg_ref[...])
    gatet_out[0] = gate.T[0:GATE_ROWS]

    pc = _dot(hb, wc_ref[...])
    gate_b = pc[:, 0:CONV_WIDTH]
    u = pc[:, CONV_WIDTH:2 * CONV_WIDTH] * pc[:, 2 * CONV_WIDTH:3 * CONV_WIDTH]

    @pl.when(i == 0)
    def _():
        u_scr[0:HALO, :] = jnp.zeros((HALO, CONV_WIDTH), F32)

    u_scr[HALO:HALO + tm, :] = u
    cw = cw_ref[...]
    conv = gate_b * (cw[0:1] * u_scr[pl.ds(HALO - 2, tm), :] + cw[1:2] * u_scr[pl.ds(HALO - 1, tm), :]
                     + cw[2:3] * u)
    u_scr[0:HALO, :] = u_scr[tm:tm + HALO, :]
    for j in range(CONV_WIDTH // MXU_TILE):
        sl = slice(MXU_TILE * j, MXU_TILE * (j + 1))
        conv_out[0, :, sl] = _group_rms(conv[:, sl], gmat, cg_ref[:, sl]).astype(BF16)


def _in_proj(x, mod3, n1, wq, wkv, wgl, wc, bg, qg, kg, cw, cg, gmat, qfeat, ktab):
    bsz, s, d = x.shape
    tm = TM_PROJ
    const = lambda shape: pl.BlockSpec(shape, lambda b, i: (0,) * len(shape))
    tok = lambda w: pl.BlockSpec((1, tm, w), lambda b, i: (b, i, 0))
    tab = lambda w: pl.BlockSpec((tm, w), lambda b, i: (i, 0))
    kspec = lambda w: pl.BlockSpec((1, N_KV, tm, w), lambda b, i: (b, 0, i, 0))
    vtspec = pl.BlockSpec((1, N_KV, tm // K_CHUNK, V_ROWS, K_CHUNK), lambda b, i: (b, 0, i, 0, 0))
    vt_shape = jax.ShapeDtypeStruct((bsz, N_KV, s // K_CHUNK, V_ROWS, K_CHUNK), BF16)
    return pl.pallas_call(
        _inproj_kernel,
        out_shape=(jax.ShapeDtypeStruct((bsz, N_HEADS, LANES, s), BF16),
                   jax.ShapeDtypeStruct((bsz, s, 2 * KV_COLS), F32),
                   jax.ShapeDtypeStruct((bsz, N_KV, s, K_SEL_WIDTH), BF16), vt_shape,
                   jax.ShapeDtypeStruct((bsz, N_KV, s, LANES), BF16), vt_shape,
                   jax.ShapeDtypeStruct((bsz, GATE_ROWS, s), F32),
                   jax.ShapeDtypeStruct((bsz, s, CONV_WIDTH), BF16)),
        grid=(bsz, s // tm),
        in_specs=[tok(d),
                  pl.BlockSpec((1, 6, d), lambda b, i: (b, 0, 0)),
                  const(n1.shape), const(wq.shape), const(wkv.shape), const(wgl.shape), const(wc.shape),
                  const(bg.shape), const(qg.shape), const(kg.shape), const(cw.shape), const(cg.shape),
                  const(gmat.shape), tab(qfeat.shape[1]), tab(ktab.shape[1])],
        out_specs=(pl.BlockSpec((1, N_HEADS, LANES, tm), lambda b, i: (b, 0, 0, i)),
                   tok(2 * KV_COLS), kspec(K_SEL_WIDTH), vtspec, kspec(LANES), vtspec,
                   pl.BlockSpec((1, GATE_ROWS, tm), lambda b, i: (b, 0, i)),
                   tok(CONV_WIDTH)),
        scratch_shapes=[pltpu.VMEM((tm + HALO, CONV_WIDTH), F32)],
        compiler_params=pltpu.CompilerParams(dimension_semantics=("arbitrary", "arbitrary"),
                                             vmem_limit_bytes=VMEM_LIMIT),
        name="in_proj",
    )(x, mod3, n1, wq, wkv, wgl, wc, bg, qg, kg, cw, cg, gmat, qfeat, ktab)


def _cmp_kernel(k_ref, pos_ref, w1_ref, w2_ref, g_ref, ctab_ref, kc_out, vct_out):
    ncp = k_ref.shape[2]

    def mlp(j):
        a = k_ref[j, 0]
        pos = pos_ref[j]
        ha = _dot((a + pos[0:1]).astype(BF16), w1_ref[j, 0])
        hb = _dot((a + pos[1:2]).astype(BF16), w1_ref[j, 1])
        h = ha + pltpu.roll(hb, ncp - 1, axis=0)
        return _dot(jax.nn.gelu(h).astype(BF16), w2_ref[j])

    ck = mlp(0)
    ckn = ck * lax.rsqrt(jnp.mean(ck * ck, axis=-1, keepdims=True) + EPS) * g_ref[...]
    kc_out[0] = jnp.concatenate([ckn.astype(BF16), ctab_ref[...]], axis=-1)
    cv = mlp(1)
    cv_t = jnp.concatenate([cv, jnp.zeros((ncp, LANES - HEAD_DIM), F32)], axis=-1).T
    vct_out[0, 0:HEAD_DIM, :] = cv_t[0:HEAD_DIM].astype(BF16)
    vct_out[0, HEAD_DIM:V_ROWS, :] = jnp.where(
        lax.broadcasted_iota(jnp.int32, (BF16_SUBLANES, ncp), 0) == 0, 1.0, 0.0).astype(BF16)


def _compress(k16, pos2, w1, w2, g, ctab):
    _, nbk, ncp, w = k16.shape
    full = lambda a: pl.BlockSpec(a.shape, lambda j: (0,) * a.ndim)
    return pl.pallas_call(
        _cmp_kernel,
        out_shape=(jax.ShapeDtypeStruct((nbk, ncp, LANES), BF16),
                   jax.ShapeDtypeStruct((nbk, V_ROWS, ncp), BF16)),
        grid=(nbk,),
        in_specs=[pl.BlockSpec((2, 1, ncp, w), lambda j: (0, j, 0, 0)),
                  full(pos2), full(w1), full(w2), full(g), full(ctab)],
        out_specs=(pl.BlockSpec((1, ncp, LANES), lambda j: (j, 0, 0)),
                   pl.BlockSpec((1, V_ROWS, ncp), lambda j: (j, 0, 0))),
        compiler_params=pltpu.CompilerParams(dimension_semantics=("arbitrary",),
                                             vmem_limit_bytes=VMEM_LIMIT),
        name="compress",
    )(k16, pos2, w1, w2, g, ctab)


def _flash_chunk(state, qa, k, vt, mask, n_rep):
    m, acc = state
    s = _dot(k, qa)
    if mask is not None:
        t = mask.shape[1]
        s = jnp.concatenate([jnp.where(mask, s[:, t * h:t * (h + 1)], NEG) for h in range(n_rep)], axis=1)
    m_new = jnp.maximum(m, jnp.max(s, axis=0, keepdims=True))
    p = jnp.exp(s - m_new)
    alpha = jnp.exp(m - m_new)
    return m_new, alpha * acc + _dot(vt, p.astype(BF16))


def _attn_kernel(qt_ref, gt_ref, kc_ref, vct_ref, ks_ref, vst_ref, kw_ref, vwt_ref, ovl_ref, ogt_ref,
                 o_ref, score_scr):
    T = qt_ref.shape[3]
    KC = K_CHUNK
    NB, NCP = ovl_ref.shape
    nsel = min(SEL_TOPK, NB)
    i = pl.program_id(1)
    t0 = i * T
    diag = (t0 + T - 1) // KC

    tcol = lax.broadcasted_iota(jnp.int32, (KC, T), 1)
    krow = lax.broadcasted_iota(jnp.int32, (KC, T), 0)
    rel = tcol - krow

    cend = CMP_STRIDE * lax.broadcasted_iota(jnp.int32, (NCP, T), 0) + (CMP_LEN - 1)
    mask_c = cend <= t0 + lax.broadcasted_iota(jnp.int32, (NCP, T), 1)

    blk = lax.broadcasted_iota(jnp.int32, (NB, T), 0)
    cur = (t0 + lax.broadcasted_iota(jnp.int32, (NB, T), 1)) // SEL_BLOCK
    valid = blk <= cur
    forced = (blk == 0) | (blk == cur) | (blk == cur - 1)
    row8 = lax.broadcasted_iota(jnp.int32, (8, T), 0)

    q_win, q_sel, oc_t = [], [], []
    for g in range(N_KV):
        qt4 = jnp.concatenate([qt_ref[0, GQA * g + h] for h in range(GQA)], axis=1)
        q_win.append(qt4)

        s = _dot(kc_ref[0, g], qt4)
        ps = []
        for h in range(GQA):
            sh = jnp.where(mask_c, s[:, T * h:T * (h + 1)], NEG)
            m = jnp.max(sh, axis=0, keepdims=True)
            m = jnp.where(m > 0.5 * NEG, m, 0.0)
            e = jnp.where(mask_c, jnp.exp(sh - m), 0.0)
            ps.append(e / jnp.maximum(jnp.sum(e, axis=0, keepdims=True), TINY))
        oc_t.append(_dot(vct_ref[0, g], jnp.concatenate(ps, axis=1).astype(BF16)))

        psum = ((ps[0] + ps[1]) + ps[2]) + ps[3]
        p_hi = psum.astype(BF16)
        p_lo = (psum - p_hi.astype(F32)).astype(BF16)
        ovl = ovl_ref[...]
        imp_t = _dot(ovl, p_hi) + _dot(ovl, p_lo)
        score = jnp.where(forced, FORCE_SCORE, jnp.where(valid, imp_t, -1.0))
        score_scr[...] = score
        n_grp = NB // 8
        grp = [score[8 * r:8 * (r + 1)] for r in range(n_grp)]
        rank = [jnp.zeros((8, T), jnp.int32) for _ in range(n_grp)]
        for ii in range(NB):
            si = score_scr[pl.ds(ii, 1), :]
            r0 = ii // 8
            for r in range(n_grp):
                ge = jnp.where(si >= grp[r], 1, 0)
                gt = jnp.where(si > grp[r], 1, 0)
                if r > r0:
                    beats = ge
                elif r < r0:
                    beats = gt
                else:
                    beats = jnp.where(row8 + 8 * r > ii, ge, gt)
                rank[r] = rank[r] + beats
        sel = (jnp.concatenate(rank, axis=0) < nsel) & valid
        selb = jnp.where(sel, 0.0, NEG)
        if NB < LANES:
            selb = jnp.concatenate([selb, jnp.full((LANES - NB, T), NEG, F32)], axis=0)
        selb = selb.astype(BF16)
        q_sel.append(jnp.concatenate([qt4, jnp.concatenate([selb] * GQA, axis=1)], axis=0))

    def init_state():
        return (jnp.full((1, GQA * T), 2.0 * NEG, F32), jnp.zeros((V_ROWS, GQA * T), F32))

    def load_kv(k_ref, vt_ref, g, c):
        k0 = pl.multiple_of(c * KC, KC)
        return k_ref[0, g, pl.ds(k0, KC), :], vt_ref[0, g, c]

    def sel_body(c, carry):
        out = []
        for g in range(N_KV):
            k, vt = load_kv(ks_ref, vst_ref, g, c)
            out.append(_flash_chunk(carry[g], q_sel[g], k, vt, None, GQA))
        return tuple(out)

    sel_state = lax.fori_loop(0, diag, sel_body, tuple(init_state() for _ in range(N_KV)))
    dist_d = rel + (t0 - diag * KC)
    mask_d = dist_d >= 0
    os_t, ow_t = [], []
    for g in range(N_KV):
        k, vt = load_kv(ks_ref, vst_ref, g, diag)
        _, acc = _flash_chunk(sel_state[g], q_sel[g], k, vt, mask_d, GQA)
        os_t.append(acc[0:HEAD_DIM] / jnp.maximum(acc[HEAD_DIM:HEAD_DIM + 1], TINY))

    win_masks, win_chunks = [], []
    for back in (2, 1, 0):
        c = diag - back
        dist = rel + (t0 - c * KC) + jnp.where(c >= 0, 0, 2 * WINDOW)
        win_masks.append((dist >= 0) & (dist < WINDOW))
        win_chunks.append(jnp.maximum(c, 0))
    for g in range(N_KV):
        st = init_state()
        for mask, c in zip(win_masks, win_chunks):
            k, vt = load_kv(kw_ref, vwt_ref, g, c)
            st = _flash_chunk(st, q_win[g], k, vt, mask, GQA)
        acc = st[1]
        ow_t.append(acc[0:HEAD_DIM] / jnp.maximum(acc[HEAD_DIM:HEAD_DIM + 1], TINY))

    gt = gt_ref[0]
    ogt = ogt_ref[...]
    pieces = []
    for hh in range(N_HEADS):
        g, h = divmod(hh, GQA)
        sl = slice(T * h, T * (h + 1))
        o = (gt[hh:hh + 1] * oc_t[g][0:HEAD_DIM, sl] + gt[N_HEADS + hh:N_HEADS + hh + 1] * os_t[g][:, sl]
             + gt[2 * N_HEADS + hh:2 * N_HEADS + hh + 1] * ow_t[g][:, sl])
        msq = jnp.mean(o * o, axis=0, keepdims=True)
        pieces.append(o * lax.rsqrt(msq + EPS) * ogt[:, hh:hh + 1])
    o_ref[0] = jnp.concatenate(pieces, axis=0).T.astype(BF16)


def _attention(qt, gatet, kc, vct, ks, vst, kw, vwt, ovl, ogt):
    bsz, _, _, s = qt.shape
    T = Q_TILE
    ncp = kc.shape[2]
    nb = ovl.shape[0]
    full4 = lambda a: pl.BlockSpec((1,) + a.shape[1:], lambda b, i: (b,) + (0,) * (a.ndim - 1))
    return pl.pallas_call(
        _attn_kernel,
        out_shape=jax.ShapeDtypeStruct((bsz, s, ATTN_WIDTH), BF16),
        grid=(bsz, s // T),
        in_specs=[pl.BlockSpec((1, N_HEADS, LANES, T), lambda b, i: (b, 0, 0, i)),
                  pl.BlockSpec((1, GATE_ROWS, T), lambda b, i: (b, 0, i)),
                  full4(kc), full4(vct), full4(ks), full4(vst), full4(kw), full4(vwt),
                  pl.BlockSpec(ovl.shape, lambda b, i: (0, 0)),
                  pl.BlockSpec(ogt.shape, lambda b, i: (0, 0))],
        out_specs=pl.BlockSpec((1, T, ATTN_WIDTH), lambda b, i: (b, i, 0)),
        scratch_shapes=[pltpu.VMEM((nb, T), F32)],
        compiler_params=pltpu.CompilerParams(dimension_semantics=("arbitrary", "arbitrary"),
                                             vmem_limit_bytes=VMEM_LIMIT),
        name="nsa_attn",
    )(qt, gatet, kc, vct, ks, vst, kw, vwt, ovl, ogt)


def _ffn_kernel(x_ref, a_ref, cv_ref, mod_ref, n2_ref, wo_ref, wg_ref, wu_ref, wd_ref, cw_ref,
                o_ref, h2_scr, acc_scr, gp_scr):
    i = pl.program_id(1)
    tm = x_ref.shape[1]
    n_chunks = wg_ref.shape[0]
    mod = mod_ref[0]
    mix = _dot(a_ref[0], wo_ref[0:ATTN_WIDTH, :]) + _dot(cv_ref[0], wo_ref[ATTN_WIDTH:ATTN_WIDTH + CONV_WIDTH, :])
    x1 = x_ref[0] + mod[2:3] * mix
    o_ref[0] = x1
    ms = jnp.mean(x1 * x1, axis=-1, keepdims=True)
    h2 = (x1 * lax.rsqrt(ms + EPS) * n2_ref[...]) * (1.0 + mod[4:5]) + mod[3:4]

    @pl.when(i == 0)
    def _():
        h2_scr[0:HALO, :] = jnp.zeros((HALO, h2_scr.shape[1]), BF16)

    h2_scr[HALO:HALO + tm, :] = h2.astype(BF16)
    acc_scr[...] = jnp.zeros(acc_scr.shape, F32)

    def body(c, carry):
        gp_scr[...] = _dot(h2_scr[...], wg_ref[c])
        cw = cw_ref[c]
        g_pre = (cw[0:1] * gp_scr[pl.ds(HALO - 2, tm), :] + cw[1:2] * gp_scr[pl.ds(HALO - 1, tm), :]
                 + cw[2:3] * gp_scr[pl.ds(HALO, tm), :])
        up = _dot(h2_scr[pl.ds(HALO, tm), :], wu_ref[c])
        act = (jax.nn.silu(g_pre) * up).astype(BF16)
        acc_scr[...] += _dot(act, wd_ref[c])
        return carry

    lax.fori_loop(0, n_chunks, body, 0)
    o_ref[0] = o_ref[0] + mod[5:6] * acc_scr[...]
    h2_scr[0:HALO, :] = h2_scr[tm:tm + HALO, :]


def _out_ffn(x, attn_n, conv_n, mod3, n2, wo, wg3, wu3, wd3, cw3):
    bsz, s, d = x.shape
    tm = TM_FFN
    resident = lambda shape: pl.BlockSpec(shape, lambda b, i: (0,) * len(shape), pipeline_mode=pl.Buffered(1))
    tok = lambda w: pl.BlockSpec((1, tm, w), lambda b, i: (b, i, 0))
    return pl.pallas_call(
        _ffn_kernel,
        out_shape=jax.ShapeDtypeStruct((bsz, s, d), F32),
        grid=(bsz, s // tm),
        in_specs=[tok(d), tok(ATTN_WIDTH), tok(CONV_WIDTH),
                  pl.BlockSpec((1, 6, d), lambda b, i: (b, 0, 0)),
                  resident(n2.shape), resident(wo.shape), resident(wg3.shape), resident(wu3.shape),
                  resident(wd3.shape), resident(cw3.shape)],
        out_specs=tok(d),
        scratch_shapes=[pltpu.VMEM((tm + HALO, d), BF16),
                        pltpu.VMEM((tm, d), F32),
                        pltpu.VMEM((tm + HALO, FF_CHUNK), F32)],
        compiler_params=pltpu.CompilerParams(dimension_semantics=("arbitrary", "arbitrary"),
                                             vmem_limit_bytes=VMEM_LIMIT),
        name="out_ffn",
    )(x, attn_n, conv_n, mod3, n2, wo, wg3, wu3, wd3, cw3)


def _pos_feats(pos):
    return np.stack([pos // SEL_BLOCK, pos % SEL_BLOCK, np.ones_like(pos), np.ones_like(pos)], axis=-1)


def _static_tables(s):
    ncp = s // CMP_STRIDE
    n_cmp = (s - CMP_LEN) // CMP_STRIDE + 1
    nb = s // SEL_BLOCK
    cs = np.arange(ncp) * CMP_STRIDE
    bs = np.arange(nb) * SEL_BLOCK
    ovl = ((cs[None, :] < bs[:, None] + SEL_BLOCK) & (cs[None, :] + CMP_LEN > bs[:, None])
           & (np.arange(ncp)[None, :] < n_cmp)).astype(np.float32)
    gmat = (np.arange(MXU_TILE)[:, None] // HEAD_DIM == np.arange(MXU_TILE)[None, :] // HEAD_DIM)

    t = np.arange(s)
    qfeat = np.zeros((s, N_HEADS, HEAD_DIM), np.float32)
    for hh in range(N_HEADS):
        slope = 2.0 ** (-8.0 * (hh + 1) / N_HEADS)
        qfeat[:, hh, 0] = slope * SEL_BLOCK
        qfeat[:, hh, 1] = slope
        qfeat[:, hh, 2] = -slope * SEL_BLOCK * (t // SEL_BLOCK)
        qfeat[:, hh, 3] = -slope * (t % SEL_BLOCK)
    ktab = np.zeros((s, K_SEL_WIDTH - HEAD_DIM), np.float32)
    ktab[:, 0:POS_FEATS] = _pos_feats(t)
    ktab[t, HEAD_DIM + t // SEL_BLOCK] = 1.0
    ctab = np.zeros((ncp, LANES - HEAD_DIM), np.float32)
    ctab[:, 0:POS_FEATS] = _pos_feats(cs + CMP_LEN - 1)
    as_bf16 = lambda a: jnp.asarray(a, BF16)
    return (as_bf16(ovl), as_bf16(gmat.astype(np.float32)), as_bf16(qfeat.reshape(s, ATTN_WIDTH)),
            as_bf16(ktab), as_bf16(ctab))


def kernel(x, c, w_ada, b_ada, norm1_g, w_in, b_gate, q_norm_g, k_norm_cmp_g, k_norm_slc_g, k_norm_win_g,
           pos_cmp_k, pos_cmp_v, w_cmp_k1, w_cmp_k2, w_cmp_v1, w_cmp_v2, conv_mix_w, attn_out_g, conv_out_g,
           w_out, norm2_g, w_ffn_gate, w_ffn_up, conv_ffn_w, w_ffn_down):
    bsz, s, d = x.shape
    assert w_ada.shape[0] == 1, "single layer"
    assert s % TM_PROJ == 0 and s % TM_FFN == 0 and s % K_CHUNK == 0 and s % Q_TILE == 0
    assert s // SEL_BLOCK <= LANES and (s // SEL_BLOCK) % 8 == 0
    d_ff = w_ffn_gate.shape[-1]
    assert d_ff % FF_CHUNK == 0
    n_ch = d_ff // FF_CHUNK
    ovl, gmat, qfeat, ktab, ctab = _static_tables(s)

    mod = _ada(c, w_ada[0], b_ada[0][None, :])
    mod3 = mod.reshape(bsz, 6, d)

    w = w_in[0]
    o_q, o_kv, o_gl = ATTN_WIDTH, ATTN_WIDTH + 6 * KV_COLS, ATTN_WIDTH + 6 * KV_COLS + N_GATE
    wq = w[:, :o_q].astype(BF16)
    wkv = w[:, o_q:o_kv].astype(BF16)
    perm = np.array([kv * GQA * 3 + h * 3 + br for br in range(3) for kv in range(N_KV) for h in range(GQA)])
    wgl = jnp.pad(w[:, o_kv:o_gl][:, perm], ((0, 0), (0, LANES - N_GATE))).astype(BF16)
    bg = jnp.pad(b_gate[0][perm], (0, LANES - N_GATE))[None, :]
    wc = w[:, o_gl:].astype(BF16)
    qg = jnp.tile(q_norm_g[0], GROUPS_PER_MXU_TILE)[None, :]
    kg = jnp.concatenate([jnp.tile(k_norm_slc_g[0], N_KV), jnp.tile(k_norm_win_g[0], N_KV)])[None, :]

    qt, kvc_raw, ks, vst, kw, vwt, gatet, conv_n = _in_proj(
        x, mod3, norm1_g, wq, wkv, wgl, wc, bg, qg, kg, conv_mix_w[0], conv_out_g, gmat, qfeat, ktab)

    ncp = s // CMP_STRIDE
    k16 = kvc_raw.reshape(bsz, s, 2, N_KV, HEAD_DIM).transpose(2, 0, 3, 1, 4)
    k16 = k16.reshape(2, bsz * N_KV, ncp, CMP_STRIDE * HEAD_DIM)
    half = CMP_STRIDE * HEAD_DIM
    pos2 = jnp.stack([pos_cmp_k[0].reshape(2, half), pos_cmp_v[0].reshape(2, half)])
    w1 = jnp.stack([w_cmp_k1[0], w_cmp_v1[0]]).reshape(2, 2, half, -1).astype(BF16)
    w2 = jnp.stack([w_cmp_k2[0], w_cmp_v2[0]]).astype(BF16)
    kc, vct = _compress(k16, pos2, w1, w2, k_norm_cmp_g, ctab)
    kc = kc.reshape(bsz, N_KV, ncp, LANES)
    vct = vct.reshape(bsz, N_KV, V_ROWS, ncp)

    attn_n = _attention(qt, gatet, kc, vct, ks, vst, kw, vwt, ovl, attn_out_g[0].reshape(N_HEADS, HEAD_DIM).T)

    wo = w_out[0].astype(BF16)
    wg3 = w_ffn_gate[0].reshape(d, n_ch, FF_CHUNK).transpose(1, 0, 2).astype(BF16)
    wu3 = w_ffn_up[0].reshape(d, n_ch, FF_CHUNK).transpose(1, 0, 2).astype(BF16)
    wd3 = w_ffn_down[0].reshape(n_ch, FF_CHUNK, d).astype(BF16)
    cw3 = jnp.pad(conv_ffn_w[0], ((0, 8 - conv_ffn_w.shape[1]), (0, 0)))
    cw3 = cw3.reshape(8, n_ch, FF_CHUNK).transpose(1, 0, 2)
    return _out_ffn(x, attn_n, conv_n, mod3, norm2_g, wo, wg3, wu3, wd3, cw3)
```

```python
import numpy as np
import jax
import jax.numpy as jnp
from jax import lax
from jax.experimental import pallas as pl
from jax.experimental.pallas import tpu as pltpu

F32 = jnp.float32
BF16 = jnp.bfloat16

N_HEADS = 8
N_KV = 2
GQA = N_HEADS // N_KV
HEAD_DIM = 64
ATTN_WIDTH = N_HEADS * HEAD_DIM
KV_COLS = N_KV * HEAD_DIM
N_GATE = 3 * N_HEADS
CMP_LEN = 32
CMP_STRIDE = 16
SEL_BLOCK = 64
SEL_TOPK = 16
WINDOW = 512
FORCE_SCORE = 1.0e4
CONV_WIDTH = 512
EPS = 1e-6
NEG = -1.0e30
TINY = float(np.finfo(np.float32).tiny)

LANES = 128
BF16_SUBLANES = 16
GROUPS_PER_MXU_TILE = 4
MXU_TILE = GROUPS_PER_MXU_TILE * HEAD_DIM
VMEM_LIMIT = 56 * 1024 * 1024

TM_PROJ = 512
TM_FFN = 512
FF_CHUNK = 256
Q_TILE = 128
K_CHUNK = 256
HALO = 16

POS_FEATS = 4
GATE_ROWS = 32
V_ROWS = HEAD_DIM + BF16_SUBLANES
K_SEL_WIDTH = 2 * LANES


def _dot(a, b):
    return jnp.dot(a, b, preferred_element_type=F32)


def _group_rms(v, gmat, gain):
    ssq = _dot((v * v).astype(BF16), gmat)
    return v * lax.rsqrt(ssq * (1.0 / HEAD_DIM) + EPS) * gain


def _ada_kernel(c_ref, w_ref, b_ref, o_ref):
    a = jax.nn.silu(c_ref[...])
    o_ref[...] = jnp.dot(a, w_ref[...], preferred_element_type=F32,
                         precision=lax.Precision.HIGHEST) + b_ref[...]


def _ada(c, w, b):
    bsz, d = c.shape
    n = w.shape[1]
    return pl.pallas_call(
        _ada_kernel,
        out_shape=jax.ShapeDtypeStruct((bsz, n), F32),
        grid=(n // d,),
        in_specs=[pl.BlockSpec((bsz, d), lambda j: (0, 0)),
                  pl.BlockSpec((d, d), lambda j: (0, j)),
                  pl.BlockSpec((1, d), lambda j: (0, j))],
        out_specs=pl.BlockSpec((bsz, d), lambda j: (0, j)),
        compiler_params=pltpu.CompilerParams(dimension_semantics=("arbitrary",),
                                             vmem_limit_bytes=VMEM_LIMIT),
        name="ada",
    )(c, w, b)


def _inproj_kernel(x_ref, mod_ref, n1_ref, wq_ref, wkv_ref, wgl_ref, wc_ref, bg_ref, qg_ref, kg_ref,
                   cw_ref, cg_ref, gmat_ref, qfeat_ref, ktab_ref,
                   qt_out, kvc_out, ks_out, vst_out, kw_out, vwt_out, gatet_out, conv_out, u_scr):
    i = pl.program_id(1)
    tm = x_ref.shape[1]
    x = x_ref[0]
    mod = mod_ref[0]
    ms = jnp.mean(x * x, axis=-1, keepdims=True)
    y = x * lax.rsqrt(ms + EPS) * n1_ref[...]
    h = y * (1.0 + mod[1:2]) + mod[0:1]
    hb = h.astype(BF16)
    gmat = gmat_ref[...]

    pq = _dot(hb, wq_ref[...])
    qfeat = qfeat_ref[...].astype(F32)
    for j in range(ATTN_WIDTH // MXU_TILE):
        qn = _group_rms(pq[:, MXU_TILE * j:MXU_TILE * (j + 1)], gmat, qg_ref[...]) * (HEAD_DIM ** -0.5)
        for hl in range(GROUPS_PER_MXU_TILE):
            hh = GROUPS_PER_MXU_TILE * j + hl
            qa = jnp.concatenate([qn[:, HEAD_DIM * hl:HEAD_DIM * (hl + 1)],
                                  qfeat[:, HEAD_DIM * hh:HEAD_DIM * (hh + 1)]], axis=-1)
            qt_out[0, hh] = qa.T.astype(BF16)

    pkv = _dot(hb, wkv_ref[...])
    kvc_out[0] = pkv[:, 0:2 * KV_COLS]
    ksw = jnp.concatenate([pkv[:, 2 * KV_COLS:3 * KV_COLS], pkv[:, 4 * KV_COLS:5 * KV_COLS]], axis=-1)
    kswn = _group_rms(ksw, gmat, kg_ref[...])
    ktab = ktab_ref[...].astype(F32)
    vs_t = pkv[:, 3 * KV_COLS:4 * KV_COLS].T.astype(BF16)
    vw_t = pkv[:, 5 * KV_COLS:6 * KV_COLS].T.astype(BF16)
    kc_n = K_CHUNK
    ones_rows = jnp.where(lax.broadcasted_iota(jnp.int32, (BF16_SUBLANES, kc_n), 0) == 0, 1.0, 0.0).astype(BF16)
    for g in range(N_KV):
        ks_out[0, g] = jnp.concatenate([kswn[:, HEAD_DIM * g:HEAD_DIM * (g + 1)], ktab], axis=-1).astype(BF16)
        kw_out[0, g] = jnp.concatenate([kswn[:, KV_COLS + HEAD_DIM * g:KV_COLS + HEAD_DIM * (g + 1)],
                                        ktab[:, 0:HEAD_DIM]], axis=-1).astype(BF16)
        for j in range(tm // kc_n):
            cols = slice(kc_n * j, kc_n * (j + 1))
            vst_out[0, g, j, 0:HEAD_DIM, :] = vs_t[HEAD_DIM * g:HEAD_DIM * (g + 1), cols]
            vst_out[0, g, j, HEAD_DIM:V_ROWS, :] = ones_rows
            vwt_out[0, g, j, 0:HEAD_DIM, :] = vw_t[HEAD_DIM * g:HEAD_DIM * (g + 1), cols]
            vwt_out[0, g, j, HEAD_DIM:V_ROWS, :] = ones_rows

    gate = jax.nn.sigmoid(_dot(hb, wgl_ref[...]) + bg_ref[...])
    gatet_out[0] = gate.T[0:GATE_ROWS]

    pc = _dot(hb, wc_ref[...])
    gate_b = pc[:, 0:CONV_WIDTH]
    u = pc[:, CONV_WIDTH:2 * CONV_WIDTH] * pc[:, 2 * CONV_WIDTH:3 * CONV_WIDTH]

    @pl.when(i == 0)
    def _():
        u_scr[0:HALO, :] = jnp.zeros((HALO, CONV_WIDTH), F32)

    u_scr[HALO:HALO + tm, :] = u
    cw = cw_ref[...]
    conv = gate_b * (cw[0:1] * u_scr[pl.ds(HALO - 2, tm), :] + cw[1:2] * u_scr[pl.ds(HALO - 1, tm), :]
                     + cw[2:3] * u)
    u_scr[0:HALO, :] = u_scr[tm:tm + HALO, :]
    for j in range(CONV_WIDTH // MXU_TILE):
        sl = slice(MXU_TILE * j, MXU_TILE * (j + 1))
        conv_out[0, :, sl] = _group_rms(conv[:, sl], gmat, cg_ref[:, sl]).astype(BF16)


def _in_proj(x, mod3, n1, wq, wkv, wgl, wc, bg, qg, kg, cw, cg, gmat, qfeat, ktab):
    bsz, s, d = x.shape
    tm = TM_PROJ
    const = lambda shape: pl.BlockSpec(shape, lambda b, i: (0,) * len(shape))
    tok = lambda w: pl.BlockSpec((1, tm, w), lambda b, i: (b, i, 0))
    tab = lambda w: pl.BlockSpec((tm, w), lambda b, i: (i, 0))
    kspec = lambda w: pl.BlockSpec((1, N_KV, tm, w), lambda b, i: (b, 0, i, 0))
    vtspec = pl.BlockSpec((1, N_KV, tm // K_CHUNK, V_ROWS, K_CHUNK), lambda b, i: (b, 0, i, 0, 0))
    vt_shape = jax.ShapeDtypeStruct((bsz, N_KV, s // K_CHUNK, V_ROWS, K_CHUNK), BF16)
    return pl.pallas_call(
        _inproj_kernel,
        out_shape=(jax.ShapeDtypeStruct((bsz, N_HEADS, LANES, s), BF16),
                   jax.ShapeDtypeStruct((bsz, s, 2 * KV_COLS), F32),
                   jax.ShapeDtypeStruct((bsz, N_KV, s, K_SEL_WIDTH), BF16), vt_shape,
                   jax.ShapeDtypeStruct((bsz, N_KV, s, LANES), BF16), vt_shape,
                   jax.ShapeDtypeStruct((bsz, GATE_ROWS, s), F32),
                   jax.ShapeDtypeStruct((bsz, s, CONV_WIDTH), BF16)),
        grid=(bsz, s // tm),
        in_specs=[tok(d),
                  pl.BlockSpec((1, 6, d), lambda b, i: (b, 0, 0)),
                  const(n1.shape), const(wq.shape), const(wkv.shape), const(wgl.shape), const(wc.shape),
                  const(bg.shape), const(qg.shape), const(kg.shape), const(cw.shape), const(cg.shape),
                  const(gmat.shape), tab(qfeat.shape[1]), tab(ktab.shape[1])],
        out_specs=(pl.BlockSpec((1, N_HEADS, LANES, tm), lambda b, i: (b, 0, 0, i)),
                   tok(2 * KV_COLS), kspec(K_SEL_WIDTH), vtspec, kspec(LANES), vtspec,
                   pl.BlockSpec((1, GATE_ROWS, tm), lambda b, i: (b, 0, i)),
                   tok(CONV_WIDTH)),
        scratch_shapes=[pltpu.VMEM((tm + HALO, CONV_WIDTH), F32)],
        compiler_params=pltpu.CompilerParams(dimension_semantics=("arbitrary", "arbitrary"),
                                             vmem_limit_bytes=VMEM_LIMIT),
        name="in_proj",
    )(x, mod3, n1, wq, wkv, wgl, wc, bg, qg, kg, cw, cg, gmat, qfeat, ktab)


def _cmp_kernel(k_ref, pos_ref, w1_ref, w2_ref, g_ref, ctab_ref, kc_out, vct_out):
    ncp = k_ref.shape[2]

    def mlp(j):
        a = k_ref[j, 0]
        pos = pos_ref[j]
        ha = _dot((a + pos[0:1]).astype(BF16), w1_ref[j, 0])
        hb = _dot((a + pos[1:2]).astype(BF16), w1_ref[j, 1])
        h = ha + pltpu.roll(hb, ncp - 1, axis=0)
        return _dot(jax.nn.gelu(h).astype(BF16), w2_ref[j])

    ck = mlp(0)
    ckn = ck * lax.rsqrt(jnp.mean(ck * ck, axis=-1, keepdims=True) + EPS) * g_ref[...]
    kc_out[0] = jnp.concatenate([ckn.astype(BF16), ctab_ref[...]], axis=-1)
    cv = mlp(1)
    cv_t = jnp.concatenate([cv, jnp.zeros((ncp, LANES - HEAD_DIM), F32)], axis=-1).T
    vct_out[0, 0:HEAD_DIM, :] = cv_t[0:HEAD_DIM].astype(BF16)
    vct_out[0, HEAD_DIM:V_ROWS, :] = jnp.where(
        lax.broadcasted_iota(jnp.int32, (BF16_SUBLANES, ncp), 0) == 0, 1.0, 0.0).astype(BF16)


def _compress(k16, pos2, w1, w2, g, ctab):
    _, nbk, ncp, w = k16.shape
    full = lambda a: pl.BlockSpec(a.shape, lambda j: (0,) * a.ndim)
    return pl.pallas_call(
        _cmp_kernel,
        out_shape=(jax.ShapeDtypeStruct((nbk, ncp, LANES), BF16),
                   jax.ShapeDtypeStruct((nbk, V_ROWS, ncp), BF16)),
        grid=(nbk,),
        in_specs=[pl.BlockSpec((2, 1, ncp, w), lambda j: (0, j, 0, 0)),
                  full(pos2), full(w1), full(w2), full(g), full(ctab)],
        out_specs=(pl.BlockSpec((1, ncp, LANES), lambda j: (j, 0, 0)),
                   pl.BlockSpec((1, V_ROWS, ncp), lambda j: (j, 0, 0))),
        compiler_params=pltpu.CompilerParams(dimension_semantics=("arbitrary",),
                                             vmem_limit_bytes=VMEM_LIMIT),
        name="compress",
    )(k16, pos2, w1, w2, g, ctab)


def _softmax_step(state, s, vt, mask, n_rep):
    m, acc = state
    if mask is not None:
        t = mask.shape[1]
        s = jnp.concatenate([jnp.where(mask, s[:, t * h:t * (h + 1)], NEG) for h in range(n_rep)], axis=1)
    m_new = jnp.maximum(m, jnp.max(s, axis=0, keepdims=True))
    p = jnp.exp(s - m_new)
    alpha = jnp.exp(m - m_new)
    return m_new, alpha * acc + _dot(vt, p.astype(BF16))


def _attn_kernel(qt_ref, gt_ref, kc_ref, vct_ref, ks_ref, vst_ref, kw_ref, vwt_ref, ovl_ref, ogt_ref,
                 o_ref, score_scr):
    T = qt_ref.shape[3]
    KC = K_CHUNK
    NB, NCP = ovl_ref.shape
    nsel = min(SEL_TOPK, NB)
    i = pl.program_id(1)
    t0 = i * T
    diag = (t0 + T - 1) // KC

    tcol = lax.broadcasted_iota(jnp.int32, (KC, T), 1)
    krow = lax.broadcasted_iota(jnp.int32, (KC, T), 0)
    rel = tcol - krow

    cend = CMP_STRIDE * lax.broadcasted_iota(jnp.int32, (NCP, T), 0) + (CMP_LEN - 1)
    mask_c = cend <= t0 + lax.broadcasted_iota(jnp.int32, (NCP, T), 1)

    blk = lax.broadcasted_iota(jnp.int32, (NB, T), 0)
    cur = (t0 + lax.broadcasted_iota(jnp.int32, (NB, T), 1)) // SEL_BLOCK
    valid = blk <= cur
    forced = (blk == 0) | (blk == cur) | (blk == cur - 1)
    row8 = lax.broadcasted_iota(jnp.int32, (8, T), 0)

    win_masks, win_chunks = [], []
    for back in (2, 1, 0):
        c = diag - back
        dist = rel + (t0 - c * KC) + jnp.where(c >= 0, 0, 2 * WINDOW)
        win_masks.append((dist >= 0) & (dist < WINDOW))
        win_chunks.append(jnp.maximum(c, 0))

    def init_state():
        return (jnp.full((1, GQA * T), 2.0 * NEG, F32), jnp.zeros((V_ROWS, GQA * T), F32))

    def key_rows(k_ref, g, c):
        return k_ref[0, g, pl.ds(pl.multiple_of(c * KC, KC), KC), :]

    qt4 = [jnp.concatenate([qt_ref[0, GQA * g + h] for h in range(GQA)], axis=1) for g in range(N_KV)]
    oc_t, q_sel = [None] * N_KV, [None] * N_KV
    win_state = [init_state() for _ in range(N_KV)]

    def finish_compressed(g, s):
        ps = []
        for h in range(GQA):
            sh = jnp.where(mask_c, s[:, T * h:T * (h + 1)], NEG)
            m = jnp.max(sh, axis=0, keepdims=True)
            m = jnp.where(m > 0.5 * NEG, m, 0.0)
            e = jnp.where(mask_c, jnp.exp(sh - m), 0.0)
            ps.append(e / jnp.maximum(jnp.sum(e, axis=0, keepdims=True), TINY))
        oc_t[g] = _dot(vct_ref[0, g], jnp.concatenate(ps, axis=1).astype(BF16))

        psum = ((ps[0] + ps[1]) + ps[2]) + ps[3]
        p_hi = psum.astype(BF16)
        p_lo = (psum - p_hi.astype(F32)).astype(BF16)
        ovl = ovl_ref[...]
        imp_t = _dot(ovl, p_hi) + _dot(ovl, p_lo)
        score = jnp.where(forced, FORCE_SCORE, jnp.where(valid, imp_t, -1.0))
        score_scr[g] = score
        n_grp = NB // 8
        grp = [score[8 * r:8 * (r + 1)] for r in range(n_grp)]
        rank = [jnp.zeros((8, T), jnp.int32) for _ in range(n_grp)]
        for ii in range(NB):
            si = score_scr[g, pl.ds(ii, 1), :]
            r0 = ii // 8
            for r in range(n_grp):
                ge = jnp.where(si >= grp[r], 1, 0)
                gt = jnp.where(si > grp[r], 1, 0)
                if r > r0:
                    beats = ge
                elif r < r0:
                    beats = gt
                else:
                    beats = jnp.where(row8 + 8 * r > ii, ge, gt)
                rank[r] = rank[r] + beats
        sel = (jnp.concatenate(rank, axis=0) < nsel) & valid
        selb = jnp.where(sel, 0.0, NEG)
        if NB < LANES:
            selb = jnp.concatenate([selb, jnp.full((LANES - NB, T), NEG, F32)], axis=0)
        selb = selb.astype(BF16)
        q_sel[g] = jnp.concatenate([qt4[g], jnp.concatenate([selb] * GQA, axis=1)], axis=0)

    def finish_window(g, j, s):
        win_state[g] = _softmax_step(win_state[g], s, vwt_ref[0, g, win_chunks[j]], win_masks[j], GQA)

    units = [("cmp", g, 0) for g in range(N_KV)] + [("win", g, j) for j in range(3) for g in range(N_KV)]
    lookahead = 2
    scores = []
    for u, (kind, g, j) in enumerate(units):
        while len(scores) < min(len(units), u + 1 + lookahead):
            kind_n, g_n, j_n = units[len(scores)]
            keys = kc_ref[0, g_n] if kind_n == "cmp" else key_rows(kw_ref, g_n, win_chunks[j_n])
            scores.append(_dot(keys, qt4[g_n]))
        if kind == "cmp":
            finish_compressed(g, scores[u])
        else:
            finish_window(g, j, scores[u])

    def sel_scores(g, c):
        return _dot(key_rows(ks_ref, g, c), q_sel[g])

    def sel_body(c, carry):
        s_cur, states = carry
        s_next, new_states = [], []
        for g in range(N_KV):
            s_next.append(sel_scores(g, c + 1))
            new_states.append(_softmax_step(states[g], s_cur[g], vst_ref[0, g, c], None, GQA))
        return tuple(s_next), tuple(new_states)

    s_diag, sel_state = lax.fori_loop(
        0, diag, sel_body,
        (tuple(sel_scores(g, 0) for g in range(N_KV)), tuple(init_state() for _ in range(N_KV))))
    mask_d = rel + (t0 - diag * KC) >= 0
    os_t, ow_t = [], []
    for g in range(N_KV):
        _, acc = _softmax_step(sel_state[g], s_diag[g], vst_ref[0, g, diag], mask_d, GQA)
        os_t.append(acc[0:HEAD_DIM] / jnp.maximum(acc[HEAD_DIM:HEAD_DIM + 1], TINY))
        acc = win_state[g][1]
        ow_t.append(acc[0:HEAD_DIM] / jnp.maximum(acc[HEAD_DIM:HEAD_DIM + 1], TINY))

    gt = gt_ref[0]
    ogt = ogt_ref[...]
    pieces = []
    for hh in range(N_HEADS):
        g, h = divmod(hh, GQA)
        sl = slice(T * h, T * (h + 1))
        o = (gt[hh:hh + 1] * oc_t[g][0:HEAD_DIM, sl] + gt[N_HEADS + hh:N_HEADS + hh + 1] * os_t[g][:, sl]
             + gt[2 * N_HEADS + hh:2 * N_HEADS + hh + 1] * ow_t[g][:, sl])
        msq = jnp.mean(o * o, axis=0, keepdims=True)
        pieces.append(o * lax.rsqrt(msq + EPS) * ogt[:, hh:hh + 1])
    o_ref[0] = jnp.concatenate(pieces, axis=0).T.astype(BF16)


def _attention(qt, gatet, kc, vct, ks, vst, kw, vwt, ovl, ogt):
    bsz, _, _, s = qt.shape
    T = Q_TILE
    ncp = kc.shape[2]
    nb = ovl.shape[0]
    full4 = lambda a: pl.BlockSpec((1,) + a.shape[1:], lambda b, i: (b,) + (0,) * (a.ndim - 1))
    return pl.pallas_call(
        _attn_kernel,
        out_shape=jax.ShapeDtypeStruct((bsz, s, ATTN_WIDTH), BF16),
        grid=(bsz, s // T),
        in_specs=[pl.BlockSpec((1, N_HEADS, LANES, T), lambda b, i: (b, 0, 0, i)),
                  pl.BlockSpec((1, GATE_ROWS, T), lambda b, i: (b, 0, i)),
                  full4(kc), full4(vct), full4(ks), full4(vst), full4(kw), full4(vwt),
                  pl.BlockSpec(ovl.shape, lambda b, i: (0, 0)),
                  pl.BlockSpec(ogt.shape, lambda b, i: (0, 0))],
        out_specs=pl.BlockSpec((1, T, ATTN_WIDTH), lambda b, i: (b, i, 0)),
        scratch_shapes=[pltpu.VMEM((N_KV, nb, T), F32)],
        compiler_params=pltpu.CompilerParams(dimension_semantics=("arbitrary", "arbitrary"),
                                             vmem_limit_bytes=VMEM_LIMIT),
        name="nsa_attn",
    )(qt, gatet, kc, vct, ks, vst, kw, vwt, ovl, ogt)


def _ffn_kernel(x_ref, a_ref, cv_ref, mod_ref, n2_ref, wo_ref, wg_ref, wu_ref, wd_ref, cw_ref,
                o_ref, h2_scr, gp_scr, act_scr):
    i = pl.program_id(1)
    tm = x_ref.shape[1]
    n_chunks = wg_ref.shape[0]
    mod = mod_ref[0]
    mix = _dot(a_ref[0], wo_ref[0:ATTN_WIDTH, :]) + _dot(cv_ref[0], wo_ref[ATTN_WIDTH:ATTN_WIDTH + CONV_WIDTH, :])
    x1 = x_ref[0] + mod[2:3] * mix
    o_ref[0] = x1
    ms = jnp.mean(x1 * x1, axis=-1, keepdims=True)
    h2 = (x1 * lax.rsqrt(ms + EPS) * n2_ref[...]) * (1.0 + mod[4:5]) + mod[3:4]

    @pl.when(i == 0)
    def _():
        h2_scr[0:HALO, :] = jnp.zeros((HALO, h2_scr.shape[1]), BF16)

    h2_scr[HALO:HALO + tm, :] = h2.astype(BF16)

    for c in range(n_chunks):
        slot = c % 2
        gp_scr[slot] = _dot(h2_scr[...], wg_ref[c])
        cw = cw_ref[c]
        g_pre = (cw[0:1] * gp_scr[slot, pl.ds(HALO - 2, tm), :] + cw[1:2] * gp_scr[slot, pl.ds(HALO - 1, tm), :]
                 + cw[2:3] * gp_scr[slot, pl.ds(HALO, tm), :])
        up = _dot(h2_scr[pl.ds(HALO, tm), :], wu_ref[c])
        act_scr[:, FF_CHUNK * c:FF_CHUNK * (c + 1)] = (jax.nn.silu(g_pre) * up).astype(BF16)

    o_ref[0] = o_ref[0] + mod[5:6] * _dot(act_scr[...], wd_ref[...])
    h2_scr[0:HALO, :] = h2_scr[tm:tm + HALO, :]


def _out_ffn(x, attn_n, conv_n, mod3, n2, wo, wg3, wu3, wd, cw3):
    bsz, s, d = x.shape
    tm = TM_FFN
    resident = lambda shape: pl.BlockSpec(shape, lambda b, i: (0,) * len(shape), pipeline_mode=pl.Buffered(1))
    tok = lambda w: pl.BlockSpec((1, tm, w), lambda b, i: (b, i, 0))
    return pl.pallas_call(
        _ffn_kernel,
        out_shape=jax.ShapeDtypeStruct((bsz, s, d), F32),
        grid=(bsz, s // tm),
        in_specs=[tok(d), tok(ATTN_WIDTH), tok(CONV_WIDTH),
                  pl.BlockSpec((1, 6, d), lambda b, i: (b, 0, 0)),
                  resident(n2.shape), resident(wo.shape), resident(wg3.shape), resident(wu3.shape),
                  resident(wd.shape), resident(cw3.shape)],
        out_specs=tok(d),
        scratch_shapes=[pltpu.VMEM((tm + HALO, d), BF16),
                        pltpu.VMEM((2, tm + HALO, FF_CHUNK), F32),
                        pltpu.VMEM((tm, wd.shape[0]), BF16)],
        compiler_params=pltpu.CompilerParams(dimension_semantics=("arbitrary", "arbitrary"),
                                             vmem_limit_bytes=VMEM_LIMIT),
        name="out_ffn",
    )(x, attn_n, conv_n, mod3, n2, wo, wg3, wu3, wd, cw3)


def _pos_feats(pos):
    return np.stack([pos // SEL_BLOCK, pos % SEL_BLOCK, np.ones_like(pos), np.ones_like(pos)], axis=-1)


def _static_tables(s):
    ncp = s // CMP_STRIDE
    n_cmp = (s - CMP_LEN) // CMP_STRIDE + 1
    nb = s // SEL_BLOCK
    cs = np.arange(ncp) * CMP_STRIDE
    bs = np.arange(nb) * SEL_BLOCK
    ovl = ((cs[None, :] < bs[:, None] + SEL_BLOCK) & (cs[None, :] + CMP_LEN > bs[:, None])
           & (np.arange(ncp)[None, :] < n_cmp)).astype(np.float32)
    gmat = (np.arange(MXU_TILE)[:, None] // HEAD_DIM == np.arange(MXU_TILE)[None, :] // HEAD_DIM)

    t = np.arange(s)
    qfeat = np.zeros((s, N_HEADS, HEAD_DIM), np.float32)
    for hh in range(N_HEADS):
        slope = 2.0 ** (-8.0 * (hh + 1) / N_HEADS)
        qfeat[:, hh, 0] = slope * SEL_BLOCK
        qfeat[:, hh, 1] = slope
        qfeat[:, hh, 2] = -slope * SEL_BLOCK * (t // SEL_BLOCK)
        qfeat[:, hh, 3] = -slope * (t % SEL_BLOCK)
    ktab = np.zeros((s, K_SEL_WIDTH - HEAD_DIM), np.float32)
    ktab[:, 0:POS_FEATS] = _pos_feats(t)
    ktab[t, HEAD_DIM + t // SEL_BLOCK] = 1.0
    ctab = np.zeros((ncp, LANES - HEAD_DIM), np.float32)
    ctab[:, 0:POS_FEATS] = _pos_feats(cs + CMP_LEN - 1)
    as_bf16 = lambda a: jnp.asarray(a, BF16)
    return (as_bf16(ovl), as_bf16(gmat.astype(np.float32)), as_bf16(qfeat.reshape(s, ATTN_WIDTH)),
            as_bf16(ktab), as_bf16(ctab))


def kernel(x, c, w_ada, b_ada, norm1_g, w_in, b_gate, q_norm_g, k_norm_cmp_g, k_norm_slc_g, k_norm_win_g,
           pos_cmp_k, pos_cmp_v, w_cmp_k1, w_cmp_k2, w_cmp_v1, w_cmp_v2, conv_mix_w, attn_out_g, conv_out_g,
           w_out, norm2_g, w_ffn_gate, w_ffn_up, conv_ffn_w, w_ffn_down):
    bsz, s, d = x.shape
    assert w_ada.shape[0] == 1, "single layer"
    assert s % TM_PROJ == 0 and s % TM_FFN == 0 and s % K_CHUNK == 0 and s % Q_TILE == 0
    assert s // SEL_BLOCK <= LANES and (s // SEL_BLOCK) % 8 == 0
    d_ff = w_ffn_gate.shape[-1]
    assert d_ff % FF_CHUNK == 0
    n_ch = d_ff // FF_CHUNK
    ovl, gmat, qfeat, ktab, ctab = _static_tables(s)

    mod = _ada(c, w_ada[0], b_ada[0][None, :])
    mod3 = mod.reshape(bsz, 6, d)

    w = w_in[0]
    o_q, o_kv, o_gl = ATTN_WIDTH, ATTN_WIDTH + 6 * KV_COLS, ATTN_WIDTH + 6 * KV_COLS + N_GATE
    wq = w[:, :o_q].astype(BF16)
    wkv = w[:, o_q:o_kv].astype(BF16)
    perm = np.array([kv * GQA * 3 + h * 3 + br for br in range(3) for kv in range(N_KV) for h in range(GQA)])
    wgl = jnp.pad(w[:, o_kv:o_gl][:, perm], ((0, 0), (0, LANES - N_GATE))).astype(BF16)
    bg = jnp.pad(b_gate[0][perm], (0, LANES - N_GATE))[None, :]
    wc = w[:, o_gl:].astype(BF16)
    qg = jnp.tile(q_norm_g[0], GROUPS_PER_MXU_TILE)[None, :]
    kg = jnp.concatenate([jnp.tile(k_norm_slc_g[0], N_KV), jnp.tile(k_norm_win_g[0], N_KV)])[None, :]

    qt, kvc_raw, ks, vst, kw, vwt, gatet, conv_n = _in_proj(
        x, mod3, norm1_g, wq, wkv, wgl, wc, bg, qg, kg, conv_mix_w[0], conv_out_g, gmat, qfeat, ktab)

    ncp = s // CMP_STRIDE
    k16 = kvc_raw.reshape(bsz, s, 2, N_KV, HEAD_DIM).transpose(2, 0, 3, 1, 4)
    k16 = k16.reshape(2, bsz * N_KV, ncp, CMP_STRIDE * HEAD_DIM)
    half = CMP_STRIDE * HEAD_DIM
    pos2 = jnp.stack([pos_cmp_k[0].reshape(2, half), pos_cmp_v[0].reshape(2, half)])
    w1 = jnp.stack([w_cmp_k1[0], w_cmp_v1[0]]).reshape(2, 2, half, -1).astype(BF16)
    w2 = jnp.stack([w_cmp_k2[0], w_cmp_v2[0]]).astype(BF16)
    kc, vct = _compress(k16, pos2, w1, w2, k_norm_cmp_g, ctab)
    kc = kc.reshape(bsz, N_KV, ncp, LANES)
    vct = vct.reshape(bsz, N_KV, V_ROWS, ncp)

    attn_n = _attention(qt, gatet, kc, vct, ks, vst, kw, vwt, ovl, attn_out_g[0].reshape(N_HEADS, HEAD_DIM).T)

    wo = w_out[0].astype(BF16)
    wg3 = w_ffn_gate[0].reshape(d, n_ch, FF_CHUNK).transpose(1, 0, 2).astype(BF16)
    wu3 = w_ffn_up[0].reshape(d, n_ch, FF_CHUNK).transpose(1, 0, 2).astype(BF16)
    wd = w_ffn_down[0].astype(BF16)
    cw3 = jnp.pad(conv_ffn_w[0], ((0, 8 - conv_ffn_w.shape[1]), (0, 0)))
    cw3 = cw3.reshape(8, n_ch, FF_CHUNK).transpose(1, 0, 2)
    return _out_ffn(x, attn_n, conv_n, mod3, norm2_g, wo, wg3, wu3, wd, cw3)
```

```python
import numpy as np
import jax
import jax.numpy as jnp
from jax import lax
from jax.experimental import pallas as pl
from jax.experimental.pallas import tpu as pltpu

F32 = jnp.float32
BF16 = jnp.bfloat16

N_HEADS = 8
N_KV = 2
GQA = N_HEADS // N_KV
HEAD_DIM = 64
ATTN_WIDTH = N_HEADS * HEAD_DIM
KV_COLS = N_KV * HEAD_DIM
N_GATE = 3 * N_HEADS
CMP_LEN = 32
CMP_STRIDE = 16
SEL_BLOCK = 64
SEL_TOPK = 16
WINDOW = 512
FORCE_SCORE = 1.0e4
CONV_WIDTH = 512
EPS = 1e-6
NEG = -1.0e30
TINY = float(np.finfo(np.float32).tiny)

LANES = 128
BF16_SUBLANES = 16
GROUPS_PER_MXU_TILE = 4
MXU_TILE = GROUPS_PER_MXU_TILE * HEAD_DIM
VMEM_LIMIT = 56 * 1024 * 1024

TM_PROJ = 512
TM_FFN = 512
FF_CHUNK = 256
Q_TILE = 128
K_CHUNK = 256
HALO = 16

POS_FEATS = 4
GATE_ROWS = 32
V_ROWS = HEAD_DIM + BF16_SUBLANES
K_SEL_WIDTH = 2 * LANES


def _dot(a, b):
    return jnp.dot(a, b, preferred_element_type=F32)


def _group_rms(v, gmat, gain):
    ssq = _dot((v * v).astype(BF16), gmat)
    return v * lax.rsqrt(ssq * (1.0 / HEAD_DIM) + EPS) * gain


def _ada_kernel(c_ref, w_ref, b_ref, o_ref):
    a = jax.nn.silu(c_ref[...])
    o_ref[...] = jnp.dot(a, w_ref[...], preferred_element_type=F32,
                         precision=lax.Precision.HIGHEST) + b_ref[...]


def _ada(c, w, b):
    bsz, d = c.shape
    n = w.shape[1]
    return pl.pallas_call(
        _ada_kernel,
        out_shape=jax.ShapeDtypeStruct((bsz, n), F32),
        grid=(n // d,),
        in_specs=[pl.BlockSpec((bsz, d), lambda j: (0, 0)),
                  pl.BlockSpec((d, d), lambda j: (0, j)),
                  pl.BlockSpec((1, d), lambda j: (0, j))],
        out_specs=pl.BlockSpec((bsz, d), lambda j: (0, j)),
        compiler_params=pltpu.CompilerParams(dimension_semantics=("arbitrary",),
                                             vmem_limit_bytes=VMEM_LIMIT),
        name="ada",
    )(c, w, b)


def _inproj_kernel(x_ref, mod_ref, n1_ref, wq_ref, wkv_ref, wgl_ref, wc_ref, bg_ref, qg_ref, kg_ref,
                   cw_ref, cg_ref, gmat_ref, qfeat_ref, ktab_ref,
                   qt_out, kvc_out, ks_out, vst_out, kw_out, vwt_out, gatet_out, conv_out, u_scr):
    i = pl.program_id(1)
    tm = x_ref.shape[1]
    x = x_ref[0]
    mod = mod_ref[0]
    ms = jnp.mean(x * x, axis=-1, keepdims=True)
    y = x * lax.rsqrt(ms + EPS) * n1_ref[...]
    h = y * (1.0 + mod[1:2]) + mod[0:1]
    hb = h.astype(BF16)
    gmat = gmat_ref[...]

    pq = _dot(hb, wq_ref[...])
    qfeat = qfeat_ref[...].astype(F32)
    for j in range(ATTN_WIDTH // MXU_TILE):
        qn = _group_rms(pq[:, MXU_TILE * j:MXU_TILE * (j + 1)], gmat, qg_ref[...]) * (HEAD_DIM ** -0.5)
        for hl in range(GROUPS_PER_MXU_TILE):
            hh = GROUPS_PER_MXU_TILE * j + hl
            qa = jnp.concatenate([qn[:, HEAD_DIM * hl:HEAD_DIM * (hl + 1)],
                                  qfeat[:, HEAD_DIM * hh:HEAD_DIM * (hh + 1)]], axis=-1)
            qt_out[0, hh] = qa.T.astype(BF16)

    pkv = _dot(hb, wkv_ref[...])
    for j in range(2 * N_KV):
        kvc_out[0, j] = pkv[:, HEAD_DIM * j:HEAD_DIM * (j + 1)]
    ksw = jnp.concatenate([pkv[:, 2 * KV_COLS:3 * KV_COLS], pkv[:, 4 * KV_COLS:5 * KV_COLS]], axis=-1)
    kswn = _group_rms(ksw, gmat, kg_ref[...])
    ktab = ktab_ref[...].astype(F32)
    vs_t = pkv[:, 3 * KV_COLS:4 * KV_COLS].T.astype(BF16)
    vw_t = pkv[:, 5 * KV_COLS:6 * KV_COLS].T.astype(BF16)
    kc_n = K_CHUNK
    ones_rows = jnp.where(lax.broadcasted_iota(jnp.int32, (BF16_SUBLANES, kc_n), 0) == 0, 1.0, 0.0).astype(BF16)
    for g in range(N_KV):
        ks_out[0, g] = jnp.concatenate([kswn[:, HEAD_DIM * g:HEAD_DIM * (g + 1)], ktab], axis=-1).astype(BF16)
        kw_out[0, g] = jnp.concatenate([kswn[:, KV_COLS + HEAD_DIM * g:KV_COLS + HEAD_DIM * (g + 1)],
                                        ktab[:, 0:HEAD_DIM]], axis=-1).astype(BF16)
        for j in range(tm // kc_n):
            cols = slice(kc_n * j, kc_n * (j + 1))
            vst_out[0, g, j, 0:HEAD_DIM, :] = vs_t[HEAD_DIM * g:HEAD_DIM * (g + 1), cols]
            vst_out[0, g, j, HEAD_DIM:V_ROWS, :] = ones_rows
            vwt_out[0, g, j, 0:HEAD_DIM, :] = vw_t[HEAD_DIM * g:HEAD_DIM * (g + 1), cols]
            vwt_out[0, g, j, HEAD_DIM:V_ROWS, :] = ones_rows

    gate = jax.nn.sigmoid(_dot(hb, wgl_ref[...]) + bg_ref[...])
    gatet_out[0] = gate.T[0:GATE_ROWS]

    pc = _dot(hb, wc_ref[...])
    gate_b = pc[:, 0:CONV_WIDTH]
    u = pc[:, CONV_WIDTH:2 * CONV_WIDTH] * pc[:, 2 * CONV_WIDTH:3 * CONV_WIDTH]

    @pl.when(i == 0)
    def _():
        u_scr[0:HALO, :] = jnp.zeros((HALO, CONV_WIDTH), F32)

    u_scr[HALO:HALO + tm, :] = u
    cw = cw_ref[...]
    conv = gate_b * (cw[0:1] * u_scr[pl.ds(HALO - 2, tm), :] + cw[1:2] * u_scr[pl.ds(HALO - 1, tm), :]
                     + cw[2:3] * u)
    u_scr[0:HALO, :] = u_scr[tm:tm + HALO, :]
    for j in range(CONV_WIDTH // MXU_TILE):
        sl = slice(MXU_TILE * j, MXU_TILE * (j + 1))
        conv_out[0, :, sl] = _group_rms(conv[:, sl], gmat, cg_ref[:, sl]).astype(BF16)


def _in_proj(x, mod3, n1, wq, wkv, wgl, wc, bg, qg, kg, cw, cg, gmat, qfeat, ktab):
    bsz, s, d = x.shape
    tm = TM_PROJ
    const = lambda shape: pl.BlockSpec(shape, lambda b, i: (0,) * len(shape))
    tok = lambda w: pl.BlockSpec((1, tm, w), lambda b, i: (b, i, 0))
    tab = lambda w: pl.BlockSpec((tm, w), lambda b, i: (i, 0))
    kspec = lambda w: pl.BlockSpec((1, N_KV, tm, w), lambda b, i: (b, 0, i, 0))
    vtspec = pl.BlockSpec((1, N_KV, tm // K_CHUNK, V_ROWS, K_CHUNK), lambda b, i: (b, 0, i, 0, 0))
    vt_shape = jax.ShapeDtypeStruct((bsz, N_KV, s // K_CHUNK, V_ROWS, K_CHUNK), BF16)
    return pl.pallas_call(
        _inproj_kernel,
        out_shape=(jax.ShapeDtypeStruct((bsz, N_HEADS, LANES, s), BF16),
                   jax.ShapeDtypeStruct((bsz, 2 * N_KV, s, HEAD_DIM), F32),
                   jax.ShapeDtypeStruct((bsz, N_KV, s, K_SEL_WIDTH), BF16), vt_shape,
                   jax.ShapeDtypeStruct((bsz, N_KV, s, LANES), BF16), vt_shape,
                   jax.ShapeDtypeStruct((bsz, GATE_ROWS, s), F32),
                   jax.ShapeDtypeStruct((bsz, s, CONV_WIDTH), BF16)),
        grid=(bsz, s // tm),
        in_specs=[tok(d),
                  pl.BlockSpec((1, 6, d), lambda b, i: (b, 0, 0)),
                  const(n1.shape), const(wq.shape), const(wkv.shape), const(wgl.shape), const(wc.shape),
                  const(bg.shape), const(qg.shape), const(kg.shape), const(cw.shape), const(cg.shape),
                  const(gmat.shape), tab(qfeat.shape[1]), tab(ktab.shape[1])],
        out_specs=(pl.BlockSpec((1, N_HEADS, LANES, tm), lambda b, i: (b, 0, 0, i)),
                   pl.BlockSpec((1, 2 * N_KV, tm, HEAD_DIM), lambda b, i: (b, 0, i, 0)),
                   kspec(K_SEL_WIDTH), vtspec, kspec(LANES), vtspec,
                   pl.BlockSpec((1, GATE_ROWS, tm), lambda b, i: (b, 0, i)),
                   tok(CONV_WIDTH)),
        scratch_shapes=[pltpu.VMEM((tm + HALO, CONV_WIDTH), F32)],
        compiler_params=pltpu.CompilerParams(dimension_semantics=("arbitrary", "arbitrary"),
                                             vmem_limit_bytes=VMEM_LIMIT),
        name="in_proj",
    )(x, mod3, n1, wq, wkv, wgl, wc, bg, qg, kg, cw, cg, gmat, qfeat, ktab)


def _cmp_kernel(kraw_ref, vraw_ref, pos_ref, w1_ref, w2_ref, g_ref, ctab_ref, kc_out, vct_out):
    ncp = kraw_ref.shape[2] // CMP_STRIDE

    def mlp(j):
        raw_ref = (kraw_ref, vraw_ref)[j]
        a = jnp.concatenate([raw_ref[0, 0, pl.ds(l, ncp, stride=CMP_STRIDE), :] for l in range(CMP_STRIDE)],
                            axis=-1)
        pos = pos_ref[j]
        ha = _dot((a + pos[0:1]).astype(BF16), w1_ref[j, 0])
        hb = _dot((a + pos[1:2]).astype(BF16), w1_ref[j, 1])
        h = ha + pltpu.roll(hb, ncp - 1, axis=0)
        return _dot(jax.nn.gelu(h).astype(BF16), w2_ref[j])

    ck = mlp(0)
    ckn = ck * lax.rsqrt(jnp.mean(ck * ck, axis=-1, keepdims=True) + EPS) * g_ref[...]
    kc_out[0, 0] = jnp.concatenate([ckn.astype(BF16), ctab_ref[...]], axis=-1)
    cv = mlp(1)
    cv_t = jnp.concatenate([cv, jnp.zeros((ncp, LANES - HEAD_DIM), F32)], axis=-1).T
    vct_out[0, 0, 0:HEAD_DIM, :] = cv_t[0:HEAD_DIM].astype(BF16)
    vct_out[0, 0, HEAD_DIM:V_ROWS, :] = jnp.where(
        lax.broadcasted_iota(jnp.int32, (BF16_SUBLANES, ncp), 0) == 0, 1.0, 0.0).astype(BF16)


def _compress(kvc_raw, pos2, w1, w2, g, ctab):
    bsz, _, s, _ = kvc_raw.shape
    ncp = s // CMP_STRIDE
    full = lambda a: pl.BlockSpec(a.shape, lambda b, j: (0,) * a.ndim)
    return pl.pallas_call(
        _cmp_kernel,
        out_shape=(jax.ShapeDtypeStruct((bsz, N_KV, ncp, LANES), BF16),
                   jax.ShapeDtypeStruct((bsz, N_KV, V_ROWS, ncp), BF16)),
        grid=(bsz, N_KV),
        in_specs=[pl.BlockSpec((1, 1, s, HEAD_DIM), lambda b, j: (b, j, 0, 0)),
                  pl.BlockSpec((1, 1, s, HEAD_DIM), lambda b, j: (b, N_KV + j, 0, 0)),
                  full(pos2), full(w1), full(w2), full(g), full(ctab)],
        out_specs=(pl.BlockSpec((1, 1, ncp, LANES), lambda b, j: (b, j, 0, 0)),
                   pl.BlockSpec((1, 1, V_ROWS, ncp), lambda b, j: (b, j, 0, 0))),
        compiler_params=pltpu.CompilerParams(dimension_semantics=("arbitrary", "arbitrary"),
                                             vmem_limit_bytes=VMEM_LIMIT),
        name="compress",
    )(kvc_raw, kvc_raw, pos2, w1, w2, g, ctab)


SOFTMAX_SLAB = 64


def _softmax_step(state, s, vt, mask, n_rep):
    m, acc = state
    kc = s.shape[0]
    if mask is not None:
        dist, hi = mask
        t = dist.shape[1]
        keep = {r: (dist[r:r + SOFTMAX_SLAB] >= 0) & (dist[r:r + SOFTMAX_SLAB] < hi)
                for r in range(0, kc, SOFTMAX_SLAB)}

    def slab(r):
        blk = s[r:r + SOFTMAX_SLAB, :]
        if mask is None:
            return blk
        return jnp.concatenate([jnp.where(keep[r], blk[:, t * h:t * (h + 1)], NEG) for h in range(n_rep)], axis=1)

    m_new = m
    for r in range(0, kc, SOFTMAX_SLAB):
        m_new = jnp.maximum(m_new, jnp.max(slab(r), axis=0, keepdims=True))
    p = jnp.concatenate([jnp.exp(slab(r) - m_new).astype(BF16) for r in range(0, kc, SOFTMAX_SLAB)], axis=0)
    alpha = jnp.exp(m - m_new)
    return m_new, alpha * acc + _dot(vt, p)


def _attn_kernel(qt_ref, gt_ref, kc_ref, vct_ref, ks_ref, vst_ref, kw_ref, vwt_ref, ovl_ref, ogt_ref,
                 o_ref, score_scr, qsel_scr, s_scr):
    T = qt_ref.shape[3]
    KC = K_CHUNK
    NB, NCP = ovl_ref.shape
    nsel = min(SEL_TOPK, NB)
    i = pl.program_id(1)
    t0 = i * T
    diag = (t0 + T - 1) // KC

    tcol = lax.broadcasted_iota(jnp.int32, (KC, T), 1)
    krow = lax.broadcasted_iota(jnp.int32, (KC, T), 0)
    rel = tcol - krow

    cend = CMP_STRIDE * lax.broadcasted_iota(jnp.int32, (NCP, T), 0) + (CMP_LEN - 1)
    mask_c = cend <= t0 + lax.broadcasted_iota(jnp.int32, (NCP, T), 1)

    blk = lax.broadcasted_iota(jnp.int32, (NB, T), 0)
    cur = (t0 + lax.broadcasted_iota(jnp.int32, (NB, T), 1)) // SEL_BLOCK
    valid = blk <= cur
    forced = (blk == 0) | (blk == cur) | (blk == cur - 1)
    row8 = lax.broadcasted_iota(jnp.int32, (8, T), 0)

    win_masks, win_chunks = [], []
    for back in (2, 1, 0):
        c = diag - back
        dist = rel + (t0 - c * KC) + jnp.where(c >= 0, 0, 2 * WINDOW)
        win_masks.append((dist, WINDOW))
        win_chunks.append(jnp.maximum(c, 0))

    def init_state():
        return (jnp.full((1, GQA * T), 2.0 * NEG, F32), jnp.zeros((V_ROWS, GQA * T), F32))

    def key_rows(k_ref, g, c):
        return k_ref[0, g, pl.ds(pl.multiple_of(c * KC, KC), KC), :]

    qt4 = [jnp.concatenate([qt_ref[0, GQA * g + h] for h in range(GQA)], axis=1) for g in range(N_KV)]
    oc_t, q_sel = [None] * N_KV, [None] * N_KV
    win_state = [init_state() for _ in range(N_KV)]

    def finish_compressed(g, s):
        ps = []
        for h in range(GQA):
            sh = jnp.where(mask_c, s[:, T * h:T * (h + 1)], NEG)
            m = jnp.max(sh, axis=0, keepdims=True)
            m = jnp.where(m > 0.5 * NEG, m, 0.0)
            e = jnp.exp(sh - m)
            ps.append(e / jnp.maximum(jnp.sum(e, axis=0, keepdims=True), TINY))
        oc_t[g] = _dot(vct_ref[0, g], jnp.concatenate(ps, axis=1).astype(BF16))

        psum = ((ps[0] + ps[1]) + ps[2]) + ps[3]
        p_hi = psum.astype(BF16)
        p_lo = (psum - p_hi.astype(F32)).astype(BF16)
        ovl = ovl_ref[...]
        imp_t = _dot(ovl, p_hi) + _dot(ovl, p_lo)
        score = jnp.where(forced, FORCE_SCORE, jnp.where(valid, imp_t, -1.0))
        score_scr[g] = score
        n_grp = NB // 8
        grp = [score[8 * r:8 * (r + 1)] for r in range(n_grp)]
        rank = [jnp.zeros((8, T), jnp.int32) for _ in range(n_grp)]
        for ii in range(NB):
            si = score_scr[g, pl.ds(ii, 1), :]
            r0 = ii // 8
            for r in range(n_grp):
                ge = jnp.where(si >= grp[r], 1, 0)
                gt = jnp.where(si > grp[r], 1, 0)
                if r > r0:
                    beats = ge
                elif r < r0:
                    beats = gt
                else:
                    beats = jnp.where(row8 + 8 * r > ii, ge, gt)
                rank[r] = rank[r] + beats
        sel = (jnp.concatenate(rank, axis=0) < nsel) & valid
        selb = jnp.where(sel, 0.0, NEG)
        if NB < LANES:
            selb = jnp.concatenate([selb, jnp.full((LANES - NB, T), NEG, F32)], axis=0)
        selb = selb.astype(BF16)
        q_sel[g] = jnp.concatenate([qt4[g], jnp.concatenate([selb] * GQA, axis=1)], axis=0)

    def finish_window(g, j, s):
        win_state[g] = _softmax_step(win_state[g], s, vwt_ref[0, g, win_chunks[j]], win_masks[j], GQA)

    units = [("cmp", g, 0) for g in range(N_KV)] + [("win", g, j) for j in range(3) for g in range(N_KV)]
    lookahead = 2
    scores = []
    for u, (kind, g, j) in enumerate(units):
        while len(scores) < min(len(units), u + 1 + lookahead):
            kind_n, g_n, j_n = units[len(scores)]
            keys = kc_ref[0, g_n] if kind_n == "cmp" else key_rows(kw_ref, g_n, win_chunks[j_n])
            scores.append(_dot(keys, qt4[g_n]))
        if kind == "cmp":
            finish_compressed(g, scores[u])
        else:
            finish_window(g, j, scores[u])

    for g in range(N_KV):
        qsel_scr[g] = q_sel[g]

    def sel_scores(slot, g, c):
        s_scr[slot, g] = _dot(key_rows(ks_ref, g, c), qsel_scr[g])

    last_chunk = ks_ref.shape[2] // KC - 1

    def sel_body(j, states):
        states = list(states)
        for slot in range(2):
            c = 2 * j + slot
            for g in range(N_KV):
                sel_scores(1 - slot, g, c + 1)
                states[g] = _softmax_step(states[g], s_scr.at[slot, g], vst_ref[0, g, c], None, GQA)
        return tuple(states)

    for g in range(N_KV):
        sel_scores(0, g, 0)
    n_pairs = diag // 2
    sel_state = list(lax.fori_loop(0, n_pairs, sel_body, tuple(init_state() for _ in range(N_KV))))
    for slot in range(2):
        c = 2 * n_pairs + slot
        c_load = jnp.minimum(c, last_chunk)
        causal = (rel + (t0 - c * KC), 1 << 30)
        for g in range(N_KV):
            if slot == 0:
                sel_scores(1, g, jnp.minimum(c + 1, last_chunk))
            sel_state[g] = _softmax_step(sel_state[g], s_scr.at[slot, g], vst_ref[0, g, c_load], causal, GQA)
    os_t, ow_t = [], []
    for g in range(N_KV):
        acc = sel_state[g][1]
        os_t.append(acc[0:HEAD_DIM] / jnp.maximum(acc[HEAD_DIM:HEAD_DIM + 1], TINY))
        acc = win_state[g][1]
        ow_t.append(acc[0:HEAD_DIM] / jnp.maximum(acc[HEAD_DIM:HEAD_DIM + 1], TINY))

    gt = gt_ref[0]
    ogt = ogt_ref[...]
    pieces = []
    for hh in range(N_HEADS):
        g, h = divmod(hh, GQA)
        sl = slice(T * h, T * (h + 1))
        o = (gt[hh:hh + 1] * oc_t[g][0:HEAD_DIM, sl] + gt[N_HEADS + hh:N_HEADS + hh + 1] * os_t[g][:, sl]
             + gt[2 * N_HEADS + hh:2 * N_HEADS + hh + 1] * ow_t[g][:, sl])
        msq = jnp.mean(o * o, axis=0, keepdims=True)
        pieces.append(o * lax.rsqrt(msq + EPS) * ogt[:, hh:hh + 1])
    o_ref[0] = jnp.concatenate(pieces, axis=0).T.astype(BF16)


def _attention(qt, gatet, kc, vct, ks, vst, kw, vwt, ovl, ogt):
    bsz, _, _, s = qt.shape
    T = Q_TILE
    ncp = kc.shape[2]
    nb = ovl.shape[0]
    full4 = lambda a: pl.BlockSpec((1,) + a.shape[1:], lambda b, i: (b,) + (0,) * (a.ndim - 1))
    return pl.pallas_call(
        _attn_kernel,
        out_shape=jax.ShapeDtypeStruct((bsz, s, ATTN_WIDTH), BF16),
        grid=(bsz, s // T),
        in_specs=[pl.BlockSpec((1, N_HEADS, LANES, T), lambda b, i: (b, 0, 0, i)),
                  pl.BlockSpec((1, GATE_ROWS, T), lambda b, i: (b, 0, i)),
                  full4(kc), full4(vct), full4(ks), full4(vst), full4(kw), full4(vwt),
                  pl.BlockSpec(ovl.shape, lambda b, i: (0, 0)),
                  pl.BlockSpec(ogt.shape, lambda b, i: (0, 0))],
        out_specs=pl.BlockSpec((1, T, ATTN_WIDTH), lambda b, i: (b, i, 0)),
        scratch_shapes=[pltpu.VMEM((N_KV, nb, T), F32),
                        pltpu.VMEM((N_KV, K_SEL_WIDTH, GQA * T), BF16),
                        pltpu.VMEM((2, N_KV, K_CHUNK, GQA * T), F32)],
        compiler_params=pltpu.CompilerParams(dimension_semantics=("arbitrary", "arbitrary"),
                                             vmem_limit_bytes=VMEM_LIMIT),
        name="nsa_attn",
    )(qt, gatet, kc, vct, ks, vst, kw, vwt, ovl, ogt)


def _ffn_kernel(x_ref, a_ref, cv_ref, mod_ref, n2_ref, wo_ref, wg_ref, wu_ref, wd_ref, cw_ref,
                o_ref, h2_scr, gp_scr, act_scr):
    i = pl.program_id(1)
    tm = x_ref.shape[1]
    n_chunks = wg_ref.shape[0]
    mod = mod_ref[0]
    mix = _dot(a_ref[0], wo_ref[0:ATTN_WIDTH, :]) + _dot(cv_ref[0], wo_ref[ATTN_WIDTH:ATTN_WIDTH + CONV_WIDTH, :])
    x1 = x_ref[0] + mod[2:3] * mix
    o_ref[0] = x1
    ms = jnp.mean(x1 * x1, axis=-1, keepdims=True)
    h2 = (x1 * lax.rsqrt(ms + EPS) * n2_ref[...]) * (1.0 + mod[4:5]) + mod[3:4]

    @pl.when(i == 0)
    def _():
        h2_scr[0:HALO, :] = jnp.zeros((HALO, h2_scr.shape[1]), BF16)

    h2_scr[HALO:HALO + tm, :] = h2.astype(BF16)

    for c in range(n_chunks):
        slot = c % 2
        gp_scr[slot] = _dot(h2_scr[...], wg_ref[c])
        cw = cw_ref[c]
        g_pre = (cw[0:1] * gp_scr[slot, pl.ds(HALO - 2, tm), :] + cw[1:2] * gp_scr[slot, pl.ds(HALO - 1, tm), :]
                 + cw[2:3] * gp_scr[slot, pl.ds(HALO, tm), :])
        up = _dot(h2_scr[pl.ds(HALO, tm), :], wu_ref[c])
        act_scr[:, FF_CHUNK * c:FF_CHUNK * (c + 1)] = (jax.nn.silu(g_pre) * up).astype(BF16)

    o_ref[0] = o_ref[0] + mod[5:6] * _dot(act_scr[...], wd_ref[...])
    h2_scr[0:HALO, :] = h2_scr[tm:tm + HALO, :]


def _out_ffn(x, attn_n, conv_n, mod3, n2, wo, wg3, wu3, wd, cw3):
    bsz, s, d = x.shape
    tm = TM_FFN
    resident = lambda shape: pl.BlockSpec(shape, lambda b, i: (0,) * len(shape), pipeline_mode=pl.Buffered(1))
    tok = lambda w: pl.BlockSpec((1, tm, w), lambda b, i: (b, i, 0))
    return pl.pallas_call(
        _ffn_kernel,
        out_shape=jax.ShapeDtypeStruct((bsz, s, d), F32),
        grid=(bsz, s // tm),
        in_specs=[tok(d), tok(ATTN_WIDTH), tok(CONV_WIDTH),
                  pl.BlockSpec((1, 6, d), lambda b, i: (b, 0, 0)),
                  resident(n2.shape), resident(wo.shape), resident(wg3.shape), resident(wu3.shape),
                  resident(wd.shape), resident(cw3.shape)],
        out_specs=tok(d),
        scratch_shapes=[pltpu.VMEM((tm + HALO, d), BF16),
                        pltpu.VMEM((2, tm + HALO, FF_CHUNK), F32),
                        pltpu.VMEM((tm, wd.shape[0]), BF16)],
        compiler_params=pltpu.CompilerParams(dimension_semantics=("arbitrary", "arbitrary"),
                                             vmem_limit_bytes=VMEM_LIMIT),
        name="out_ffn",
    )(x, attn_n, conv_n, mod3, n2, wo, wg3, wu3, wd, cw3)


def _pos_feats(pos):
    return np.stack([pos // SEL_BLOCK, pos % SEL_BLOCK, np.ones_like(pos), np.ones_like(pos)], axis=-1)


def _static_tables(s):
    ncp = s // CMP_STRIDE
    n_cmp = (s - CMP_LEN) // CMP_STRIDE + 1
    nb = s // SEL_BLOCK
    cs = np.arange(ncp) * CMP_STRIDE
    bs = np.arange(nb) * SEL_BLOCK
    ovl = ((cs[None, :] < bs[:, None] + SEL_BLOCK) & (cs[None, :] + CMP_LEN > bs[:, None])
           & (np.arange(ncp)[None, :] < n_cmp)).astype(np.float32)
    gmat = (np.arange(MXU_TILE)[:, None] // HEAD_DIM == np.arange(MXU_TILE)[None, :] // HEAD_DIM)

    t = np.arange(s)
    qfeat = np.zeros((s, N_HEADS, HEAD_DIM), np.float32)
    for hh in range(N_HEADS):
        slope = 2.0 ** (-8.0 * (hh + 1) / N_HEADS)
        qfeat[:, hh, 0] = slope * SEL_BLOCK
        qfeat[:, hh, 1] = slope
        qfeat[:, hh, 2] = -slope * SEL_BLOCK * (t // SEL_BLOCK)
        qfeat[:, hh, 3] = -slope * (t % SEL_BLOCK)
    ktab = np.zeros((s, K_SEL_WIDTH - HEAD_DIM), np.float32)
    ktab[:, 0:POS_FEATS] = _pos_feats(t)
    ktab[t, HEAD_DIM + t // SEL_BLOCK] = 1.0
    ctab = np.zeros((ncp, LANES - HEAD_DIM), np.float32)
    ctab[:, 0:POS_FEATS] = _pos_feats(cs + CMP_LEN - 1)
    as_bf16 = lambda a: jnp.asarray(a, BF16)
    return (as_bf16(ovl), as_bf16(gmat.astype(np.float32)), as_bf16(qfeat.reshape(s, ATTN_WIDTH)),
            as_bf16(ktab), as_bf16(ctab))


def kernel(x, c, w_ada, b_ada, norm1_g, w_in, b_gate, q_norm_g, k_norm_cmp_g, k_norm_slc_g, k_norm_win_g,
           pos_cmp_k, pos_cmp_v, w_cmp_k1, w_cmp_k2, w_cmp_v1, w_cmp_v2, conv_mix_w, attn_out_g, conv_out_g,
           w_out, norm2_g, w_ffn_gate, w_ffn_up, conv_ffn_w, w_ffn_down):
    bsz, s, d = x.shape
    assert w_ada.shape[0] == 1, "single layer"
    assert s % TM_PROJ == 0 and s % TM_FFN == 0 and s % K_CHUNK == 0 and s % Q_TILE == 0
    assert s // SEL_BLOCK <= LANES and (s // SEL_BLOCK) % 8 == 0
    d_ff = w_ffn_gate.shape[-1]
    assert d_ff % FF_CHUNK == 0
    n_ch = d_ff // FF_CHUNK
    ovl, gmat, qfeat, ktab, ctab = _static_tables(s)

    mod = _ada(c, w_ada[0], b_ada[0][None, :])
    mod3 = mod.reshape(bsz, 6, d)

    w = w_in[0]
    o_q, o_kv, o_gl = ATTN_WIDTH, ATTN_WIDTH + 6 * KV_COLS, ATTN_WIDTH + 6 * KV_COLS + N_GATE
    wq = w[:, :o_q].astype(BF16)
    wkv = w[:, o_q:o_kv].astype(BF16)
    perm = np.array([kv * GQA * 3 + h * 3 + br for br in range(3) for kv in range(N_KV) for h in range(GQA)])
    wgl = jnp.pad(w[:, o_kv:o_gl][:, perm], ((0, 0), (0, LANES - N_GATE))).astype(BF16)
    bg = jnp.pad(b_gate[0][perm], (0, LANES - N_GATE))[None, :]
    wc = w[:, o_gl:].astype(BF16)
    qg = jnp.tile(q_norm_g[0], GROUPS_PER_MXU_TILE)[None, :]
    kg = jnp.concatenate([jnp.tile(k_norm_slc_g[0], N_KV), jnp.tile(k_norm_win_g[0], N_KV)])[None, :]

    qt, kvc_raw, ks, vst, kw, vwt, gatet, conv_n = _in_proj(
        x, mod3, norm1_g, wq, wkv, wgl, wc, bg, qg, kg, conv_mix_w[0], conv_out_g, gmat, qfeat, ktab)

    half = CMP_STRIDE * HEAD_DIM
    pos2 = jnp.stack([pos_cmp_k[0].reshape(2, half), pos_cmp_v[0].reshape(2, half)])
    w1 = jnp.stack([w_cmp_k1[0], w_cmp_v1[0]]).reshape(2, 2, half, -1).astype(BF16)
    w2 = jnp.stack([w_cmp_k2[0], w_cmp_v2[0]]).astype(BF16)
    kc, vct = _compress(kvc_raw, pos2, w1, w2, k_norm_cmp_g, ctab)

    attn_n = _attention(qt, gatet, kc, vct, ks, vst, kw, vwt, ovl, attn_out_g[0].reshape(N_HEADS, HEAD_DIM).T)

    wo = w_out[0].astype(BF16)
    wg3 = w_ffn_gate[0].reshape(d, n_ch, FF_CHUNK).transpose(1, 0, 2).astype(BF16)
    wu3 = w_ffn_up[0].reshape(d, n_ch, FF_CHUNK).transpose(1, 0, 2).astype(BF16)
    wd = w_ffn_down[0].astype(BF16)
    cw3 = jnp.pad(conv_ffn_w[0], ((0, 8 - conv_ffn_w.shape[1]), (0, 0)))
    cw3 = cw3.reshape(8, n_ch, FF_CHUNK).transpose(1, 0, 2)
    return _out_ffn(x, attn_n, conv_n, mod3, norm2_g, wo, wg3, wu3, wd, cw3)
```

```python
import numpy as np
import jax
import jax.numpy as jnp
from jax import lax
from jax.experimental import pallas as pl
from jax.experimental.pallas import tpu as pltpu

F32 = jnp.float32
BF16 = jnp.bfloat16

N_HEADS = 8
N_KV = 2
GQA = N_HEADS // N_KV
HEAD_DIM = 64
ATTN_WIDTH = N_HEADS * HEAD_DIM
KV_COLS = N_KV * HEAD_DIM
N_GATE = 3 * N_HEADS
CMP_LEN = 32
CMP_STRIDE = 16
SEL_BLOCK = 64
SEL_TOPK = 16
WINDOW = 512
FORCE_SCORE = 1.0e4
CONV_WIDTH = 512
EPS = 1e-6
NEG = -1.0e30
TINY = float(np.finfo(np.float32).tiny)

LANES = 128
BF16_SUBLANES = 16
GROUPS_PER_MXU_TILE = 4
MXU_TILE = GROUPS_PER_MXU_TILE * HEAD_DIM
VMEM_LIMIT = 56 * 1024 * 1024

TM_PROJ = 512
TM_FFN = 512
FF_CHUNK = 256
Q_TILE = 128
K_CHUNK = 256
WIN_CHUNK = 128
HALO = 16

LOG2E = float(np.log2(np.e))
SLOPE_TERMS = 3
POS_FEATS = 2 * SLOPE_TERMS
GATE_ROWS = 32
V_ROWS = HEAD_DIM + BF16_SUBLANES
K_SEL_WIDTH = 2 * LANES


def _dot(a, b):
    return jnp.dot(a, b, preferred_element_type=F32)


def _group_rms(v, gmat, gain):
    ssq = _dot((v * v).astype(BF16), gmat)
    return v * lax.rsqrt(ssq * (1.0 / HEAD_DIM) + EPS) * gain


def _ada_kernel(c_ref, w_ref, b_ref, o_ref):
    a = jax.nn.silu(c_ref[...])
    o_ref[...] = jnp.dot(a, w_ref[...], preferred_element_type=F32,
                         precision=lax.Precision.HIGHEST) + b_ref[...]


def _ada(c, w, b):
    bsz, d = c.shape
    n = w.shape[1]
    return pl.pallas_call(
        _ada_kernel,
        out_shape=jax.ShapeDtypeStruct((bsz, n), F32),
        grid=(n // d,),
        in_specs=[pl.BlockSpec((bsz, d), lambda j: (0, 0)),
                  pl.BlockSpec((d, d), lambda j: (0, j)),
                  pl.BlockSpec((1, d), lambda j: (0, j))],
        out_specs=pl.BlockSpec((bsz, d), lambda j: (0, j)),
        compiler_params=pltpu.CompilerParams(dimension_semantics=("arbitrary",),
                                             vmem_limit_bytes=VMEM_LIMIT),
        name="ada",
    )(c, w, b)


def _inproj_kernel(x_ref, mod_ref, n1_ref, wq_ref, wkv_ref, wgl_ref, wc_ref, bg_ref, qg_ref, kg_ref,
                   cw_ref, cg_ref, gmat_ref, qfeat_ref, ktab_ref,
                   qt_out, kvc_out, ks_out, vst_out, kw_out, vwt_out, gatet_out, conv_out, u_scr):
    i = pl.program_id(1)
    tm = x_ref.shape[1]
    x = x_ref[0]
    mod = mod_ref[0]
    ms = jnp.mean(x * x, axis=-1, keepdims=True)
    y = x * lax.rsqrt(ms + EPS) * n1_ref[...]
    h = y * (1.0 + mod[1:2]) + mod[0:1]
    hb = h.astype(BF16)
    gmat = gmat_ref[...]

    pq = _dot(hb, wq_ref[...])
    pkv = _dot(hb, wkv_ref[...])
    pg = _dot(hb, wgl_ref[...])
    pc = _dot(hb, wc_ref[...])

    qfeat = jnp.broadcast_to(qfeat_ref[...], (tm, ATTN_WIDTH))
    for j in range(ATTN_WIDTH // MXU_TILE):
        qn = _group_rms(pq[:, MXU_TILE * j:MXU_TILE * (j + 1)], gmat, qg_ref[...]) * (HEAD_DIM ** -0.5 * LOG2E)
        for hl in range(GROUPS_PER_MXU_TILE):
            hh = GROUPS_PER_MXU_TILE * j + hl
            qa = jnp.concatenate([qn[:, HEAD_DIM * hl:HEAD_DIM * (hl + 1)],
                                  qfeat[:, HEAD_DIM * hh:HEAD_DIM * (hh + 1)]], axis=-1)
            qt_out[0, hh] = qa.T.astype(BF16)

    for j in range(2 * N_KV):
        kvc_out[0, j] = pkv[:, HEAD_DIM * j:HEAD_DIM * (j + 1)]
    ksw = jnp.concatenate([pkv[:, 2 * KV_COLS:3 * KV_COLS], pkv[:, 4 * KV_COLS:5 * KV_COLS]], axis=-1)
    kswn = _group_rms(ksw, gmat, kg_ref[...])
    ktab = ktab_ref[...].astype(F32)
    vs_t = pkv[:, 3 * KV_COLS:4 * KV_COLS].T.astype(BF16)
    vw_t = pkv[:, 5 * KV_COLS:6 * KV_COLS].T.astype(BF16)
    for g in range(N_KV):
        ks_out[0, g] = jnp.concatenate([kswn[:, HEAD_DIM * g:HEAD_DIM * (g + 1)], ktab], axis=-1).astype(BF16)
        kw_out[0, g] = jnp.concatenate([kswn[:, KV_COLS + HEAD_DIM * g:KV_COLS + HEAD_DIM * (g + 1)],
                                        ktab[:, 0:HEAD_DIM]], axis=-1).astype(BF16)
        for out, v_t, kc_n in ((vst_out, vs_t, K_CHUNK), (vwt_out, vw_t, WIN_CHUNK)):
            ones_rows = jnp.where(lax.broadcasted_iota(jnp.int32, (BF16_SUBLANES, kc_n), 0) == 0,
                                  1.0, 0.0).astype(BF16)
            for j in range(tm // kc_n):
                out[0, g, j, 0:HEAD_DIM, :] = v_t[HEAD_DIM * g:HEAD_DIM * (g + 1), kc_n * j:kc_n * (j + 1)]
                out[0, g, j, HEAD_DIM:V_ROWS, :] = ones_rows

    gate = jax.nn.sigmoid(pg + bg_ref[...])
    gatet_out[0] = gate.T[0:GATE_ROWS]

    gate_b = pc[:, 0:CONV_WIDTH]
    u = pc[:, CONV_WIDTH:2 * CONV_WIDTH] * pc[:, 2 * CONV_WIDTH:3 * CONV_WIDTH]

    @pl.when(i == 0)
    def _():
        u_scr[0:HALO, :] = jnp.zeros((HALO, CONV_WIDTH), F32)

    u_scr[HALO:HALO + tm, :] = u
    cw = cw_ref[...]
    conv = gate_b * (cw[0:1] * u_scr[pl.ds(HALO - 2, tm), :] + cw[1:2] * u_scr[pl.ds(HALO - 1, tm), :]
                     + cw[2:3] * u)
    u_scr[0:HALO, :] = u_scr[tm:tm + HALO, :]
    for j in range(CONV_WIDTH // MXU_TILE):
        sl = slice(MXU_TILE * j, MXU_TILE * (j + 1))
        conv_out[0, :, sl] = _group_rms(conv[:, sl], gmat, cg_ref[:, sl]).astype(BF16)


def _in_proj(x, mod3, n1, wq, wkv, wgl, wc, bg, qg, kg, cw, cg, gmat, qfeat, ktab):
    bsz, s, d = x.shape
    tm = TM_PROJ
    const = lambda shape: pl.BlockSpec(shape, lambda b, i: (0,) * len(shape))
    tok = lambda w: pl.BlockSpec((1, tm, w), lambda b, i: (b, i, 0))
    tab = lambda w: pl.BlockSpec((tm, w), lambda b, i: (i, 0))
    kspec = lambda w: pl.BlockSpec((1, N_KV, tm, w), lambda b, i: (b, 0, i, 0))
    vtspec = lambda kc: pl.BlockSpec((1, N_KV, tm // kc, V_ROWS, kc), lambda b, i: (b, 0, i, 0, 0))
    vt_shape = lambda kc: jax.ShapeDtypeStruct((bsz, N_KV, s // kc, V_ROWS, kc), BF16)
    return pl.pallas_call(
        _inproj_kernel,
        out_shape=(jax.ShapeDtypeStruct((bsz, N_HEADS, LANES, s), BF16),
                   jax.ShapeDtypeStruct((bsz, 2 * N_KV, s, HEAD_DIM), F32),
                   jax.ShapeDtypeStruct((bsz, N_KV, s, K_SEL_WIDTH), BF16), vt_shape(K_CHUNK),
                   jax.ShapeDtypeStruct((bsz, N_KV, s, LANES), BF16), vt_shape(WIN_CHUNK),
                   jax.ShapeDtypeStruct((bsz, GATE_ROWS, s), F32),
                   jax.ShapeDtypeStruct((bsz, s, CONV_WIDTH), BF16)),
        grid=(bsz, s // tm),
        in_specs=[tok(d),
                  pl.BlockSpec((1, 6, d), lambda b, i: (b, 0, 0)),
                  const(n1.shape), const(wq.shape), const(wkv.shape), const(wgl.shape), const(wc.shape),
                  const(bg.shape), const(qg.shape), const(kg.shape), const(cw.shape), const(cg.shape),
                  const(gmat.shape), const(qfeat.shape), tab(ktab.shape[1])],
        out_specs=(pl.BlockSpec((1, N_HEADS, LANES, tm), lambda b, i: (b, 0, 0, i)),
                   pl.BlockSpec((1, 2 * N_KV, tm, HEAD_DIM), lambda b, i: (b, 0, i, 0)),
                   kspec(K_SEL_WIDTH), vtspec(K_CHUNK), kspec(LANES), vtspec(WIN_CHUNK),
                   pl.BlockSpec((1, GATE_ROWS, tm), lambda b, i: (b, 0, i)),
                   tok(CONV_WIDTH)),
        scratch_shapes=[pltpu.VMEM((tm + HALO, CONV_WIDTH), F32)],
        compiler_params=pltpu.CompilerParams(dimension_semantics=("arbitrary", "arbitrary"),
                                             vmem_limit_bytes=VMEM_LIMIT),
        name="in_proj",
    )(x, mod3, n1, wq, wkv, wgl, wc, bg, qg, kg, cw, cg, gmat, qfeat, ktab)


def _cmp_kernel(kraw_ref, vraw_ref, pos_ref, w1_ref, w2_ref, g_ref, ctab_ref, kc_out, vct_out):
    ncp = kraw_ref.shape[2] // CMP_STRIDE

    def mlp(j):
        raw_ref = (kraw_ref, vraw_ref)[j]
        a = jnp.concatenate([raw_ref[0, 0, pl.ds(l, ncp, stride=CMP_STRIDE), :] for l in range(CMP_STRIDE)],
                            axis=-1)
        pos = pos_ref[j]
        ha = _dot((a + pos[0:1]).astype(BF16), w1_ref[j, 0])
        hb = _dot((a + pos[1:2]).astype(BF16), w1_ref[j, 1])
        h = ha + pltpu.roll(hb, ncp - 1, axis=0)
        return _dot(jax.nn.gelu(h).astype(BF16), w2_ref[j])

    ck = mlp(0)
    ckn = ck * lax.rsqrt(jnp.mean(ck * ck, axis=-1, keepdims=True) + EPS) * g_ref[...]
    kc_out[0, 0] = jnp.concatenate([ckn.astype(BF16), ctab_ref[...]], axis=-1)
    cv = mlp(1)
    cv_t = jnp.concatenate([cv, jnp.zeros((ncp, LANES - HEAD_DIM), F32)], axis=-1).T
    vct_out[0, 0, 0:HEAD_DIM, :] = cv_t[0:HEAD_DIM].astype(BF16)
    vct_out[0, 0, HEAD_DIM:V_ROWS, :] = jnp.where(
        lax.broadcasted_iota(jnp.int32, (BF16_SUBLANES, ncp), 0) == 0, 1.0, 0.0).astype(BF16)


def _compress(kvc_raw, pos2, w1, w2, g, ctab):
    bsz, _, s, _ = kvc_raw.shape
    ncp = s // CMP_STRIDE
    full = lambda a: pl.BlockSpec(a.shape, lambda b, j: (0,) * a.ndim)
    return pl.pallas_call(
        _cmp_kernel,
        out_shape=(jax.ShapeDtypeStruct((bsz, N_KV, ncp, LANES), BF16),
                   jax.ShapeDtypeStruct((bsz, N_KV, V_ROWS, ncp), BF16)),
        grid=(bsz, N_KV),
        in_specs=[pl.BlockSpec((1, 1, s, HEAD_DIM), lambda b, j: (b, j, 0, 0)),
                  pl.BlockSpec((1, 1, s, HEAD_DIM), lambda b, j: (b, N_KV + j, 0, 0)),
                  full(pos2), full(w1), full(w2), full(g), full(ctab)],
        out_specs=(pl.BlockSpec((1, 1, ncp, LANES), lambda b, j: (b, j, 0, 0)),
                   pl.BlockSpec((1, 1, V_ROWS, ncp), lambda b, j: (b, j, 0, 0))),
        compiler_params=pltpu.CompilerParams(dimension_semantics=("arbitrary", "arbitrary"),
                                             vmem_limit_bytes=VMEM_LIMIT),
        name="compress",
    )(kvc_raw, kvc_raw, pos2, w1, w2, g, ctab)


SOFTMAX_SLAB = 64


def _softmax_step(state, s, vt, mask, n_rep):
    m, acc = state
    kc = s.shape[0]
    if mask is not None:
        dist, hi = mask
        t = dist.shape[1]
        keep = {r: (dist[r:r + SOFTMAX_SLAB] >= 0) & (dist[r:r + SOFTMAX_SLAB] < hi)
                for r in range(0, kc, SOFTMAX_SLAB)}

    def slab(r):
        blk = s[r:r + SOFTMAX_SLAB, :]
        if mask is None:
            return blk
        return jnp.concatenate([jnp.where(keep[r], blk[:, t * h:t * (h + 1)], NEG) for h in range(n_rep)], axis=1)

    m_new = m
    for r in range(0, kc, SOFTMAX_SLAB):
        m_new = jnp.maximum(m_new, jnp.max(slab(r), axis=0, keepdims=True))
    p = jnp.concatenate([jnp.exp2(slab(r) - m_new).astype(BF16) for r in range(0, kc, SOFTMAX_SLAB)], axis=0)
    alpha = jnp.exp2(m - m_new)
    return m_new, alpha * acc + _dot(vt, p)


def _attn_kernel(qt_ref, gt_ref, kc_ref, vct_ref, ks_ref, vst_ref, kw_ref, vwt_ref, ovl_ref, ogt_ref,
                 o_ref, score_scr, qsel_scr, s_scr):
    T = qt_ref.shape[3]
    KC = K_CHUNK
    NB, NCP = ovl_ref.shape
    nsel = min(SEL_TOPK, NB)
    i = pl.program_id(1)
    t0 = i * T
    diag = (t0 + T - 1) // KC

    tcol = lax.broadcasted_iota(jnp.int32, (KC, T), 1)
    krow = lax.broadcasted_iota(jnp.int32, (KC, T), 0)
    rel = tcol - krow

    cend = CMP_STRIDE * lax.broadcasted_iota(jnp.int32, (NCP, T), 0) + (CMP_LEN - 1)
    mask_c = cend <= t0 + lax.broadcasted_iota(jnp.int32, (NCP, T), 1)

    blk = lax.broadcasted_iota(jnp.int32, (NB, T), 0)
    cur = (t0 + lax.broadcasted_iota(jnp.int32, (NB, T), 1)) // SEL_BLOCK
    valid = blk <= cur
    forced = (blk == 0) | (blk == cur) | (blk == cur - 1)
    row8 = lax.broadcasted_iota(jnp.int32, (8, T), 0)

    win_keys = WINDOW + T
    w0 = pl.multiple_of(jnp.maximum(t0 - WINDOW, 0), WIN_CHUNK)
    win_mask = ((t0 - w0) + lax.broadcasted_iota(jnp.int32, (win_keys, T), 1)
                - lax.broadcasted_iota(jnp.int32, (win_keys, T), 0), WINDOW)

    def init_state():
        return (jnp.full((1, GQA * T), 2.0 * NEG, F32), jnp.zeros((V_ROWS, GQA * T), F32))

    def key_rows(k_ref, g, c):
        return k_ref[0, g, pl.ds(pl.multiple_of(c * KC, KC), KC), :]

    qt4 = [jnp.concatenate([qt_ref[0, GQA * g + h] for h in range(GQA)], axis=1) for g in range(N_KV)]
    oc_t, q_sel = [None] * N_KV, [None] * N_KV
    win_state = [init_state() for _ in range(N_KV)]

    def finish_compressed(g, s):
        ps = []
        for h in range(GQA):
            sh = jnp.where(mask_c, s[:, T * h:T * (h + 1)], NEG)
            m = jnp.max(sh, axis=0, keepdims=True)
            m = jnp.where(m > 0.5 * NEG, m, 0.0)
            e = jnp.exp2(sh - m)
            ps.append(e / jnp.maximum(jnp.sum(e, axis=0, keepdims=True), TINY))
        oc_t[g] = _dot(vct_ref[0, g], jnp.concatenate(ps, axis=1).astype(BF16))

        psum = ((ps[0] + ps[1]) + ps[2]) + ps[3]
        p_hi = psum.astype(BF16)
        p_lo = (psum - p_hi.astype(F32)).astype(BF16)
        ovl = ovl_ref[...]
        imp_t = _dot(ovl, p_hi) + _dot(ovl, p_lo)
        score = jnp.where(forced, FORCE_SCORE, jnp.where(valid, imp_t, -1.0))
        score_scr[g] = score
        n_grp = NB // 8
        grp = [score[8 * r:8 * (r + 1)] for r in range(n_grp)]
        rank = [jnp.zeros((8, T), jnp.int32) for _ in range(n_grp)]
        for ii in range(NB):
            si = score_scr[g, pl.ds(ii, 1), :]
            r0 = ii // 8
            for r in range(n_grp):
                ge = jnp.where(si >= grp[r], 1, 0)
                gt = jnp.where(si > grp[r], 1, 0)
                if r > r0:
                    beats = ge
                elif r < r0:
                    beats = gt
                else:
                    beats = jnp.where(row8 + 8 * r > ii, ge, gt)
                rank[r] = rank[r] + beats
        sel = (jnp.concatenate(rank, axis=0) < nsel) & valid
        selb = jnp.where(sel, 0.0, NEG)
        if NB < LANES:
            selb = jnp.concatenate([selb, jnp.full((LANES - NB, T), NEG, F32)], axis=0)
        selb = selb.astype(BF16)
        q_sel[g] = jnp.concatenate([qt4[g], jnp.concatenate([selb] * GQA, axis=1)], axis=0)

    def finish_window(g, s):
        vt = jnp.concatenate([vwt_ref[0, g, w0 // WIN_CHUNK + j] for j in range(win_keys // WIN_CHUNK)], axis=1)
        win_state[g] = _softmax_step(win_state[g], s, vt, win_mask, GQA)

    units = [("cmp", g) for g in range(N_KV)] + [("win", g) for g in range(N_KV)]
    lookahead = 2
    scores = []
    for u, (kind, g) in enumerate(units):
        while len(scores) < min(len(units), u + 1 + lookahead):
            kind_n, g_n = units[len(scores)]
            keys = kc_ref[0, g_n] if kind_n == "cmp" else kw_ref[0, g_n, pl.ds(w0, win_keys), :]
            scores.append(_dot(keys, qt4[g_n]))
        if kind == "cmp":
            finish_compressed(g, scores[u])
        else:
            finish_window(g, scores[u])

    for g in range(N_KV):
        qsel_scr[g] = q_sel[g]

    def sel_scores(slot, g, c):
        s_scr[slot, g] = _dot(key_rows(ks_ref, g, c), qsel_scr[g])

    last_chunk = ks_ref.shape[2] // KC - 1

    def sel_body(j, states):
        states = list(states)
        for slot in range(2):
            c = 2 * j + slot
            for g in range(N_KV):
                sel_scores(1 - slot, g, c + 1)
                states[g] = _softmax_step(states[g], s_scr.at[slot, g], vst_ref[0, g, c], None, GQA)
        return tuple(states)

    for g in range(N_KV):
        sel_scores(0, g, 0)
    n_pairs = diag // 2
    sel_state = list(lax.fori_loop(0, n_pairs, sel_body, tuple(init_state() for _ in range(N_KV))))
    def tail_step(states, slot):
        c = 2 * n_pairs + slot
        c_load = jnp.minimum(c, last_chunk)
        causal = (rel + (t0 - c * KC), 1 << 30)
        return tuple(_softmax_step(states[g], s_scr.at[slot, g], vst_ref[0, g, c_load], causal, GQA)
                     for g in range(N_KV))

    for g in range(N_KV):
        sel_scores(1, g, jnp.minimum(2 * n_pairs + 1, last_chunk))
    sel_state = tail_step(tuple(sel_state), 0)
    sel_state = lax.cond(2 * n_pairs + 1 <= diag, lambda st: tail_step(st, 1), lambda st: st, sel_state)
    os_t, ow_t = [], []
    for g in range(N_KV):
        acc = sel_state[g][1]
        os_t.append(acc[0:HEAD_DIM] / jnp.maximum(acc[HEAD_DIM:HEAD_DIM + 1], TINY))
        acc = win_state[g][1]
        ow_t.append(acc[0:HEAD_DIM] / jnp.maximum(acc[HEAD_DIM:HEAD_DIM + 1], TINY))

    gt = gt_ref[0]
    ogt = ogt_ref[...]
    pieces = []
    for hh in range(N_HEADS):
        g, h = divmod(hh, GQA)
        sl = slice(T * h, T * (h + 1))
        o = (gt[hh:hh + 1] * oc_t[g][0:HEAD_DIM, sl] + gt[N_HEADS + hh:N_HEADS + hh + 1] * os_t[g][:, sl]
             + gt[2 * N_HEADS + hh:2 * N_HEADS + hh + 1] * ow_t[g][:, sl])
        msq = jnp.mean(o * o, axis=0, keepdims=True)
        pieces.append(o * lax.rsqrt(msq + EPS) * ogt[:, hh:hh + 1])
    o_ref[0] = jnp.concatenate(pieces, axis=0).T.astype(BF16)


def _attention(qt, gatet, kc, vct, ks, vst, kw, vwt, ovl, ogt):
    bsz, _, _, s = qt.shape
    T = Q_TILE
    ncp = kc.shape[2]
    nb = ovl.shape[0]
    full4 = lambda a: pl.BlockSpec((1,) + a.shape[1:], lambda b, i: (b,) + (0,) * (a.ndim - 1))
    return pl.pallas_call(
        _attn_kernel,
        out_shape=jax.ShapeDtypeStruct((bsz, s, ATTN_WIDTH), BF16),
        grid=(bsz, s // T),
        in_specs=[pl.BlockSpec((1, N_HEADS, LANES, T), lambda b, i: (b, 0, 0, i)),
                  pl.BlockSpec((1, GATE_ROWS, T), lambda b, i: (b, 0, i)),
                  full4(kc), full4(vct), full4(ks), full4(vst), full4(kw), full4(vwt),
                  pl.BlockSpec(ovl.shape, lambda b, i: (0, 0)),
                  pl.BlockSpec(ogt.shape, lambda b, i: (0, 0))],
        out_specs=pl.BlockSpec((1, T, ATTN_WIDTH), lambda b, i: (b, i, 0)),
        scratch_shapes=[pltpu.VMEM((N_KV, nb, T), F32),
                        pltpu.VMEM((N_KV, K_SEL_WIDTH, GQA * T), BF16),
                        pltpu.VMEM((2, N_KV, K_CHUNK, GQA * T), F32)],
        compiler_params=pltpu.CompilerParams(dimension_semantics=("arbitrary", "arbitrary"),
                                             vmem_limit_bytes=VMEM_LIMIT),
        name="nsa_attn",
    )(qt, gatet, kc, vct, ks, vst, kw, vwt, ovl, ogt)


def _ffn_kernel(x_ref, a_ref, cv_ref, mod_ref, n2_ref, wo_ref, wg_ref, wu_ref, wd_ref, cw_ref,
                o_ref, h2_scr, gp_scr, act_scr):
    i = pl.program_id(1)
    tm = x_ref.shape[1]
    n_chunks = wg_ref.shape[0]
    mod = mod_ref[0]
    mix = _dot(a_ref[0], wo_ref[0:ATTN_WIDTH, :]) + _dot(cv_ref[0], wo_ref[ATTN_WIDTH:ATTN_WIDTH + CONV_WIDTH, :])
    x1 = x_ref[0] + mod[2:3] * mix
    o_ref[0] = x1
    ms = jnp.mean(x1 * x1, axis=-1, keepdims=True)
    h2 = (x1 * lax.rsqrt(ms + EPS) * n2_ref[...]) * (1.0 + mod[4:5]) + mod[3:4]

    @pl.when(i == 0)
    def _():
        h2_scr[0:HALO, :] = jnp.zeros((HALO, h2_scr.shape[1]), BF16)

    h2_scr[HALO:HALO + tm, :] = h2.astype(BF16)

    for c in range(n_chunks):
        slot = c % 2
        gp_scr[slot] = _dot(h2_scr[...], wg_ref[c])
        cw = cw_ref[c]
        g_pre = (cw[0:1] * gp_scr[slot, pl.ds(HALO - 2, tm), :] + cw[1:2] * gp_scr[slot, pl.ds(HALO - 1, tm), :]
                 + cw[2:3] * gp_scr[slot, pl.ds(HALO, tm), :])
        up = _dot(h2_scr[pl.ds(HALO, tm), :], wu_ref[c])
        act_scr[:, FF_CHUNK * c:FF_CHUNK * (c + 1)] = (jax.nn.silu(g_pre) * up).astype(BF16)

    o_ref[0] = o_ref[0] + mod[5:6] * _dot(act_scr[...], wd_ref[...])
    h2_scr[0:HALO, :] = h2_scr[tm:tm + HALO, :]


def _out_ffn(x, attn_n, conv_n, mod3, n2, wo, wg3, wu3, wd, cw3):
    bsz, s, d = x.shape
    tm = TM_FFN
    resident = lambda shape: pl.BlockSpec(shape, lambda b, i: (0,) * len(shape), pipeline_mode=pl.Buffered(1))
    tok = lambda w: pl.BlockSpec((1, tm, w), lambda b, i: (b, i, 0))
    return pl.pallas_call(
        _ffn_kernel,
        out_shape=jax.ShapeDtypeStruct((bsz, s, d), F32),
        grid=(bsz, s // tm),
        in_specs=[tok(d), tok(ATTN_WIDTH), tok(CONV_WIDTH),
                  pl.BlockSpec((1, 6, d), lambda b, i: (b, 0, 0)),
                  resident(n2.shape), resident(wo.shape), resident(wg3.shape), resident(wu3.shape),
                  resident(wd.shape), resident(cw3.shape)],
        out_specs=tok(d),
        scratch_shapes=[pltpu.VMEM((tm + HALO, d), BF16),
                        pltpu.VMEM((2, tm + HALO, FF_CHUNK), F32),
                        pltpu.VMEM((tm, wd.shape[0]), BF16)],
        compiler_params=pltpu.CompilerParams(dimension_semantics=("arbitrary", "arbitrary"),
                                             vmem_limit_bytes=VMEM_LIMIT),
        name="out_ffn",
    )(x, attn_n, conv_n, mod3, n2, wo, wg3, wu3, wd, cw3)


def _pos_feats(pos):
    return np.stack([pos // SEL_BLOCK, pos % SEL_BLOCK] * SLOPE_TERMS, axis=-1)


def _bf16_round(v):
    bits = np.asarray(v, np.float32).view(np.uint32)
    bits = (bits + np.uint32(0x7FFF) + ((bits >> np.uint32(16)) & np.uint32(1))) & np.uint32(0xFFFF0000)
    return bits.view(np.float32)


def _static_tables(s):
    ncp = s // CMP_STRIDE
    n_cmp = (s - CMP_LEN) // CMP_STRIDE + 1
    nb = s // SEL_BLOCK
    cs = np.arange(ncp) * CMP_STRIDE
    bs = np.arange(nb) * SEL_BLOCK
    ovl = ((cs[None, :] < bs[:, None] + SEL_BLOCK) & (cs[None, :] + CMP_LEN > bs[:, None])
           & (np.arange(ncp)[None, :] < n_cmp)).astype(np.float32)
    gmat = (np.arange(MXU_TILE)[:, None] // HEAD_DIM == np.arange(MXU_TILE)[None, :] // HEAD_DIM)

    t = np.arange(s)
    qfeat = np.zeros((N_HEADS, HEAD_DIM), np.float32)
    for hh in range(N_HEADS):
        rest = np.float32(2.0 ** (-8.0 * (hh + 1) / N_HEADS) * LOG2E)
        for i in range(SLOPE_TERMS):
            term = _bf16_round(rest)
            rest = np.float32(rest - term)
            qfeat[hh, 2 * i] = term * SEL_BLOCK
            qfeat[hh, 2 * i + 1] = term
    ktab = np.zeros((s, K_SEL_WIDTH - HEAD_DIM), np.float32)
    ktab[:, 0:POS_FEATS] = _pos_feats(t)
    ktab[t, HEAD_DIM + t // SEL_BLOCK] = 1.0
    ctab = np.zeros((ncp, LANES - HEAD_DIM), np.float32)
    ctab[:, 0:POS_FEATS] = _pos_feats(cs + CMP_LEN - 1)
    as_bf16 = lambda a: jnp.asarray(a, BF16)
    return (as_bf16(ovl), as_bf16(gmat.astype(np.float32)), jnp.asarray(qfeat.reshape(1, ATTN_WIDTH)),
            as_bf16(ktab), as_bf16(ctab))


def kernel(x, c, w_ada, b_ada, norm1_g, w_in, b_gate, q_norm_g, k_norm_cmp_g, k_norm_slc_g, k_norm_win_g,
           pos_cmp_k, pos_cmp_v, w_cmp_k1, w_cmp_k2, w_cmp_v1, w_cmp_v2, conv_mix_w, attn_out_g, conv_out_g,
           w_out, norm2_g, w_ffn_gate, w_ffn_up, conv_ffn_w, w_ffn_down):
    bsz, s, d = x.shape
    assert w_ada.shape[0] == 1, "single layer"
    assert s % TM_PROJ == 0 and s % TM_FFN == 0 and s % K_CHUNK == 0 and s % Q_TILE == 0
    assert s // SEL_BLOCK <= LANES and (s // SEL_BLOCK) % 8 == 0 and s >= WINDOW + Q_TILE
    d_ff = w_ffn_gate.shape[-1]
    assert d_ff % FF_CHUNK == 0
    n_ch = d_ff // FF_CHUNK
    ovl, gmat, qfeat, ktab, ctab = _static_tables(s)

    mod = _ada(c, w_ada[0], b_ada[0][None, :])
    mod3 = mod.reshape(bsz, 6, d)

    w = w_in[0]
    o_q, o_kv, o_gl = ATTN_WIDTH, ATTN_WIDTH + 6 * KV_COLS, ATTN_WIDTH + 6 * KV_COLS + N_GATE
    wq = w[:, :o_q].astype(BF16)
    wkv = w[:, o_q:o_kv].astype(BF16)
    perm = np.array([kv * GQA * 3 + h * 3 + br for br in range(3) for kv in range(N_KV) for h in range(GQA)])
    wgl = jnp.pad(w[:, o_kv:o_gl][:, perm], ((0, 0), (0, LANES - N_GATE))).astype(BF16)
    bg = jnp.pad(b_gate[0][perm], (0, LANES - N_GATE))[None, :]
    wc = w[:, o_gl:].astype(BF16)
    qg = jnp.tile(q_norm_g[0], GROUPS_PER_MXU_TILE)[None, :]
    kg = jnp.concatenate([jnp.tile(k_norm_slc_g[0], N_KV), jnp.tile(k_norm_win_g[0], N_KV)])[None, :]

    qt, kvc_raw, ks, vst, kw, vwt, gatet, conv_n = _in_proj(
        x, mod3, norm1_g, wq, wkv, wgl, wc, bg, qg, kg, conv_mix_w[0], conv_out_g, gmat, qfeat, ktab)

    half = CMP_STRIDE * HEAD_DIM
    pos2 = jnp.stack([pos_cmp_k[0].reshape(2, half), pos_cmp_v[0].reshape(2, half)])
    w1 = jnp.stack([w_cmp_k1[0], w_cmp_v1[0]]).reshape(2, 2, half, -1).astype(BF16)
    w2 = jnp.stack([w_cmp_k2[0], w_cmp_v2[0]]).astype(BF16)
    kc, vct = _compress(kvc_raw, pos2, w1, w2, k_norm_cmp_g, ctab)

    attn_n = _attention(qt, gatet, kc, vct, ks, vst, kw, vwt, ovl, attn_out_g[0].reshape(N_HEADS, HEAD_DIM).T)

    wo = w_out[0].astype(BF16)
    wg3 = w_ffn_gate[0].reshape(d, n_ch, FF_CHUNK).transpose(1, 0, 2).astype(BF16)
    wu3 = w_ffn_up[0].reshape(d, n_ch, FF_CHUNK).transpose(1, 0, 2).astype(BF16)
    wd = w_ffn_down[0].astype(BF16)
    cw3 = jnp.pad(conv_ffn_w[0], ((0, 8 - conv_ffn_w.shape[1]), (0, 0)))
    cw3 = cw3.reshape(8, n_ch, FF_CHUNK).transpose(1, 0, 2)
    return _out_ffn(x, attn_n, conv_n, mod3, norm2_g, wo, wg3, wu3, wd, cw3)
```

```python
import numpy as np
import jax
import jax.numpy as jnp
from jax import lax
from jax.experimental import pallas as pl
from jax.experimental.pallas import tpu as pltpu

F32 = jnp.float32
BF16 = jnp.bfloat16

N_HEADS = 8
N_KV = 2
GQA = N_HEADS // N_KV
HEAD_DIM = 64
ATTN_WIDTH = N_HEADS * HEAD_DIM
KV_COLS = N_KV * HEAD_DIM
N_GATE = 3 * N_HEADS
CMP_LEN = 32
CMP_STRIDE = 16
SEL_BLOCK = 64
SEL_TOPK = 16
WINDOW = 512
FORCE_SCORE = 1.0e4
CONV_WIDTH = 512
EPS = 1e-6
NEG = -1.0e30
TINY = float(np.finfo(np.float32).tiny)

LANES = 128
BF16_SUBLANES = 16
GROUPS_PER_MXU_TILE = 4
MXU_TILE = GROUPS_PER_MXU_TILE * HEAD_DIM
VMEM_LIMIT = 56 * 1024 * 1024

TM_PROJ = 512
PROJ_SPLIT = 2
TM_FFN = 512
FF_CHUNK = 256
Q_TILE = 128
K_CHUNK = 256
WIN_CHUNK = 128
HALO = 16

LOG2E = float(np.log2(np.e))
SLOPE_TERMS = 3
POS_FEATS = 2 * SLOPE_TERMS
GATE_ROWS = 32
V_ROWS = HEAD_DIM + BF16_SUBLANES
K_SEL_WIDTH = 2 * LANES


def _dot(a, b):
    return jnp.dot(a, b, preferred_element_type=F32)


def _group_rms(v, gmat, gain):
    ssq = _dot((v * v).astype(BF16), gmat)
    return v * lax.rsqrt(ssq * (1.0 / HEAD_DIM) + EPS) * gain


def _ada_kernel(c_ref, w_ref, b_ref, o_ref):
    a = jax.nn.silu(c_ref[...])
    o_ref[...] = jnp.dot(a, w_ref[...], preferred_element_type=F32,
                         precision=lax.Precision.HIGHEST) + b_ref[...]


def _ada(c, w, b):
    bsz, d = c.shape
    n = w.shape[1]
    return pl.pallas_call(
        _ada_kernel,
        out_shape=jax.ShapeDtypeStruct((bsz, n), F32),
        grid=(n // d,),
        in_specs=[pl.BlockSpec((bsz, d), lambda j: (0, 0)),
                  pl.BlockSpec((d, d), lambda j: (0, j)),
                  pl.BlockSpec((1, d), lambda j: (0, j))],
        out_specs=pl.BlockSpec((bsz, d), lambda j: (0, j)),
        compiler_params=pltpu.CompilerParams(dimension_semantics=("arbitrary",),
                                             vmem_limit_bytes=VMEM_LIMIT),
        name="ada",
    )(c, w, b)


def _inproj_kernel(x_ref, mod_ref, n1_ref, wq_ref, wkv_ref, wgl_ref, wc_ref, bg_ref, qg_ref, kg_ref,
                   cw_ref, cg_ref, gmat_ref, qfeat_ref, ktab_ref,
                   qt_out, kvc_out, ks_out, vst_out, kw_out, vwt_out, gatet_out, conv_out, u_scr):
    i = pl.program_id(1)
    tm = x_ref.shape[1]
    th = tm // PROJ_SPLIT
    mod = mod_ref[0]
    gmat = gmat_ref[...]
    cw = cw_ref[...]

    @pl.when(i == 0)
    def _():
        u_scr[0:HALO, :] = jnp.zeros((HALO, CONV_WIDTH), F32)

    proj = []
    for r0 in range(0, tm, th):
        x = x_ref[0, r0:r0 + th, :]
        ms = jnp.mean(x * x, axis=-1, keepdims=True)
        y = x * lax.rsqrt(ms + EPS) * n1_ref[...]
        hb = (y * (1.0 + mod[1:2]) + mod[0:1]).astype(BF16)
        proj.append((_dot(hb, wq_ref[...]), _dot(hb, wkv_ref[...]), _dot(hb, wgl_ref[...]), _dot(hb, wc_ref[...])))

    qfeat = jnp.broadcast_to(qfeat_ref[...], (th, ATTN_WIDTH))
    for (pq, pkv, pg, pc), r0 in zip(proj, range(0, tm, th)):
        rows = slice(r0, r0 + th)
        for j in range(ATTN_WIDTH // MXU_TILE):
            qn = _group_rms(pq[:, MXU_TILE * j:MXU_TILE * (j + 1)], gmat, qg_ref[...]) * (HEAD_DIM ** -0.5 * LOG2E)
            for hl in range(GROUPS_PER_MXU_TILE):
                hh = GROUPS_PER_MXU_TILE * j + hl
                qa = jnp.concatenate([qn[:, HEAD_DIM * hl:HEAD_DIM * (hl + 1)],
                                      qfeat[:, HEAD_DIM * hh:HEAD_DIM * (hh + 1)]], axis=-1)
                qt_out[0, hh, :, rows] = qa.T.astype(BF16)

        for j in range(2 * N_KV):
            kvc_out[0, j, rows, :] = pkv[:, HEAD_DIM * j:HEAD_DIM * (j + 1)]
        ksw = jnp.concatenate([pkv[:, 2 * KV_COLS:3 * KV_COLS], pkv[:, 4 * KV_COLS:5 * KV_COLS]], axis=-1)
        kswn = _group_rms(ksw, gmat, kg_ref[...])
        ktab = ktab_ref[rows, :].astype(F32)
        vs_t = pkv[:, 3 * KV_COLS:4 * KV_COLS].T.astype(BF16)
        vw_t = pkv[:, 5 * KV_COLS:6 * KV_COLS].T.astype(BF16)
        for g in range(N_KV):
            ks_out[0, g, rows, :] = jnp.concatenate(
                [kswn[:, HEAD_DIM * g:HEAD_DIM * (g + 1)], ktab], axis=-1).astype(BF16)
            kw_out[0, g, rows, :] = jnp.concatenate(
                [kswn[:, KV_COLS + HEAD_DIM * g:KV_COLS + HEAD_DIM * (g + 1)], ktab[:, 0:HEAD_DIM]],
                axis=-1).astype(BF16)
            for out, v_t, kc_n in ((vst_out, vs_t, K_CHUNK), (vwt_out, vw_t, WIN_CHUNK)):
                ones_rows = jnp.where(lax.broadcasted_iota(jnp.int32, (BF16_SUBLANES, kc_n), 0) == 0,
                                      1.0, 0.0).astype(BF16)
                for j in range(th // kc_n):
                    cj = r0 // kc_n + j
                    out[0, g, cj, 0:HEAD_DIM, :] = v_t[HEAD_DIM * g:HEAD_DIM * (g + 1), kc_n * j:kc_n * (j + 1)]
                    out[0, g, cj, HEAD_DIM:V_ROWS, :] = ones_rows

        gate = jax.nn.sigmoid(pg + bg_ref[...])
        gatet_out[0, :, rows] = gate.T[0:GATE_ROWS]

        gate_b = pc[:, 0:CONV_WIDTH]
        u = pc[:, CONV_WIDTH:2 * CONV_WIDTH] * pc[:, 2 * CONV_WIDTH:3 * CONV_WIDTH]
        u_scr[HALO + r0:HALO + r0 + th, :] = u
        conv = gate_b * (cw[0:1] * u_scr[pl.ds(HALO + r0 - 2, th), :] + cw[1:2] * u_scr[pl.ds(HALO + r0 - 1, th), :]
                         + cw[2:3] * u)
        for j in range(CONV_WIDTH // MXU_TILE):
            sl = slice(MXU_TILE * j, MXU_TILE * (j + 1))
            conv_out[0, rows, sl] = _group_rms(conv[:, sl], gmat, cg_ref[:, sl]).astype(BF16)
    u_scr[0:HALO, :] = u_scr[tm:tm + HALO, :]


def _in_proj(x, mod3, n1, wq, wkv, wgl, wc, bg, qg, kg, cw, cg, gmat, qfeat, ktab):
    bsz, s, d = x.shape
    tm = TM_PROJ
    const = lambda shape: pl.BlockSpec(shape, lambda b, i: (0,) * len(shape))
    tok = lambda w: pl.BlockSpec((1, tm, w), lambda b, i: (b, i, 0))
    tab = lambda w: pl.BlockSpec((tm, w), lambda b, i: (i, 0))
    kspec = lambda w: pl.BlockSpec((1, N_KV, tm, w), lambda b, i: (b, 0, i, 0))
    vtspec = lambda kc: pl.BlockSpec((1, N_KV, tm // kc, V_ROWS, kc), lambda b, i: (b, 0, i, 0, 0))
    vt_shape = lambda kc: jax.ShapeDtypeStruct((bsz, N_KV, s // kc, V_ROWS, kc), BF16)
    return pl.pallas_call(
        _inproj_kernel,
        out_shape=(jax.ShapeDtypeStruct((bsz, N_HEADS, LANES, s), BF16),
                   jax.ShapeDtypeStruct((bsz, 2 * N_KV, s, HEAD_DIM), F32),
                   jax.ShapeDtypeStruct((bsz, N_KV, s, K_SEL_WIDTH), BF16), vt_shape(K_CHUNK),
                   jax.ShapeDtypeStruct((bsz, N_KV, s, LANES), BF16), vt_shape(WIN_CHUNK),
                   jax.ShapeDtypeStruct((bsz, GATE_ROWS, s), F32),
                   jax.ShapeDtypeStruct((bsz, s, CONV_WIDTH), BF16)),
        grid=(bsz, s // tm),
        in_specs=[tok(d),
                  pl.BlockSpec((1, 6, d), lambda b, i: (b, 0, 0)),
                  const(n1.shape), const(wq.shape), const(wkv.shape), const(wgl.shape), const(wc.shape),
                  const(bg.shape), const(qg.shape), const(kg.shape), const(cw.shape), const(cg.shape),
                  const(gmat.shape), const(qfeat.shape), tab(ktab.shape[1])],
        out_specs=(pl.BlockSpec((1, N_HEADS, LANES, tm), lambda b, i: (b, 0, 0, i)),
                   pl.BlockSpec((1, 2 * N_KV, tm, HEAD_DIM), lambda b, i: (b, 0, i, 0)),
                   kspec(K_SEL_WIDTH), vtspec(K_CHUNK), kspec(LANES), vtspec(WIN_CHUNK),
                   pl.BlockSpec((1, GATE_ROWS, tm), lambda b, i: (b, 0, i)),
                   tok(CONV_WIDTH)),
        scratch_shapes=[pltpu.VMEM((tm + HALO, CONV_WIDTH), F32)],
        compiler_params=pltpu.CompilerParams(dimension_semantics=("arbitrary", "arbitrary"),
                                             vmem_limit_bytes=VMEM_LIMIT),
        name="in_proj",
    )(x, mod3, n1, wq, wkv, wgl, wc, bg, qg, kg, cw, cg, gmat, qfeat, ktab)


def _cmp_kernel(kraw_ref, vraw_ref, pos_ref, w1_ref, w2_ref, g_ref, ctab_ref, kc_out, vct_out):
    ncp = kraw_ref.shape[2] // CMP_STRIDE

    def mlp(j):
        raw_ref = (kraw_ref, vraw_ref)[j]
        a = jnp.concatenate([raw_ref[0, 0, pl.ds(l, ncp, stride=CMP_STRIDE), :] for l in range(CMP_STRIDE)],
                            axis=-1)
        pos = pos_ref[j]
        ha = _dot((a + pos[0:1]).astype(BF16), w1_ref[j, 0])
        hb = _dot((a + pos[1:2]).astype(BF16), w1_ref[j, 1])
        h = ha + pltpu.roll(hb, ncp - 1, axis=0)
        return _dot(jax.nn.gelu(h).astype(BF16), w2_ref[j])

    ck = mlp(0)
    ckn = ck * lax.rsqrt(jnp.mean(ck * ck, axis=-1, keepdims=True) + EPS) * g_ref[...]
    kc_out[0, 0] = jnp.concatenate([ckn.astype(BF16), ctab_ref[...]], axis=-1)
    cv = mlp(1)
    cv_t = jnp.concatenate([cv, jnp.zeros((ncp, LANES - HEAD_DIM), F32)], axis=-1).T
    vct_out[0, 0, 0:HEAD_DIM, :] = cv_t[0:HEAD_DIM].astype(BF16)
    vct_out[0, 0, HEAD_DIM:V_ROWS, :] = jnp.where(
        lax.broadcasted_iota(jnp.int32, (BF16_SUBLANES, ncp), 0) == 0, 1.0, 0.0).astype(BF16)


def _compress(kvc_raw, pos2, w1, w2, g, ctab):
    bsz, _, s, _ = kvc_raw.shape
    ncp = s // CMP_STRIDE
    full = lambda a: pl.BlockSpec(a.shape, lambda b, j: (0,) * a.ndim)
    return pl.pallas_call(
        _cmp_kernel,
        out_shape=(jax.ShapeDtypeStruct((bsz, N_KV, ncp, LANES), BF16),
                   jax.ShapeDtypeStruct((bsz, N_KV, V_ROWS, ncp), BF16)),
        grid=(bsz, N_KV),
        in_specs=[pl.BlockSpec((1, 1, s, HEAD_DIM), lambda b, j: (b, j, 0, 0)),
                  pl.BlockSpec((1, 1, s, HEAD_DIM), lambda b, j: (b, N_KV + j, 0, 0)),
                  full(pos2), full(w1), full(w2), full(g), full(ctab)],
        out_specs=(pl.BlockSpec((1, 1, ncp, LANES), lambda b, j: (b, j, 0, 0)),
                   pl.BlockSpec((1, 1, V_ROWS, ncp), lambda b, j: (b, j, 0, 0))),
        compiler_params=pltpu.CompilerParams(dimension_semantics=("arbitrary", "arbitrary"),
                                             vmem_limit_bytes=VMEM_LIMIT),
        name="compress",
    )(kvc_raw, kvc_raw, pos2, w1, w2, g, ctab)


SOFTMAX_SLAB = 64


def _softmax_step(state, s, vt, mask, n_rep):
    m, acc = state
    kc = s.shape[0]
    if mask is not None:
        dist, hi = mask
        t = dist.shape[1]
        keep = {r: (dist[r:r + SOFTMAX_SLAB] >= 0) & (dist[r:r + SOFTMAX_SLAB] < hi)
                for r in range(0, kc, SOFTMAX_SLAB)}

    def slab(r):
        blk = s[r:r + SOFTMAX_SLAB, :]
        if mask is None:
            return blk
        return jnp.concatenate([jnp.where(keep[r], blk[:, t * h:t * (h + 1)], NEG) for h in range(n_rep)], axis=1)

    m_new = m
    for r in range(0, kc, SOFTMAX_SLAB):
        m_new = jnp.maximum(m_new, jnp.max(slab(r), axis=0, keepdims=True))
    p = jnp.concatenate([jnp.exp2(slab(r) - m_new).astype(BF16) for r in range(0, kc, SOFTMAX_SLAB)], axis=0)
    alpha = jnp.exp2(m - m_new)
    return m_new, alpha * acc + _dot(vt, p)


def _attn_kernel(qt_ref, gt_ref, kc_ref, vct_ref, ks_ref, vst_ref, kw_ref, vwt_ref, ovl_ref, ogt_ref,
                 o_ref, score_scr, qsel_scr, s_scr):
    T = qt_ref.shape[3]
    KC = K_CHUNK
    NB, NCP = ovl_ref.shape
    nsel = min(SEL_TOPK, NB)
    i = pl.program_id(1)
    t0 = i * T
    diag = (t0 + T - 1) // KC

    tcol = lax.broadcasted_iota(jnp.int32, (KC, T), 1)
    krow = lax.broadcasted_iota(jnp.int32, (KC, T), 0)
    rel = tcol - krow

    cend = CMP_STRIDE * lax.broadcasted_iota(jnp.int32, (NCP, T), 0) + (CMP_LEN - 1)
    mask_c = cend <= t0 + lax.broadcasted_iota(jnp.int32, (NCP, T), 1)

    blk = lax.broadcasted_iota(jnp.int32, (NB, T), 0)
    cur = (t0 + lax.broadcasted_iota(jnp.int32, (NB, T), 1)) // SEL_BLOCK
    valid = blk <= cur
    forced = (blk == 0) | (blk == cur) | (blk == cur - 1)
    row8 = lax.broadcasted_iota(jnp.int32, (8, T), 0)

    win_keys = WINDOW + T
    w0 = pl.multiple_of(jnp.maximum(t0 - WINDOW, 0), WIN_CHUNK)
    win_mask = ((t0 - w0) + lax.broadcasted_iota(jnp.int32, (win_keys, T), 1)
                - lax.broadcasted_iota(jnp.int32, (win_keys, T), 0), WINDOW)

    def init_state():
        return (jnp.full((1, GQA * T), 2.0 * NEG, F32), jnp.zeros((V_ROWS, GQA * T), F32))

    def key_rows(k_ref, g, c):
        return k_ref[0, g, pl.ds(pl.multiple_of(c * KC, KC), KC), :]

    qt4 = [jnp.concatenate([qt_ref[0, GQA * g + h] for h in range(GQA)], axis=1) for g in range(N_KV)]
    oc_t, q_sel = [None] * N_KV, [None] * N_KV
    win_state = [init_state() for _ in range(N_KV)]

    def finish_compressed(g, s):
        ps = []
        for h in range(GQA):
            sh = jnp.where(mask_c, s[:, T * h:T * (h + 1)], NEG)
            m = jnp.max(sh, axis=0, keepdims=True)
            m = jnp.where(m > 0.5 * NEG, m, 0.0)
            e = jnp.exp2(sh - m)
            ps.append(e / jnp.maximum(jnp.sum(e, axis=0, keepdims=True), TINY))
        oc_t[g] = _dot(vct_ref[0, g], jnp.concatenate(ps, axis=1).astype(BF16))

        psum = ((ps[0] + ps[1]) + ps[2]) + ps[3]
        p_hi = psum.astype(BF16)
        p_lo = (psum - p_hi.astype(F32)).astype(BF16)
        ovl = ovl_ref[...]
        imp_t = _dot(ovl, p_hi) + _dot(ovl, p_lo)
        score = jnp.where(forced, FORCE_SCORE, jnp.where(valid, imp_t, -1.0))
        score_scr[g] = score
        n_grp = NB // 8
        grp = [score[8 * r:8 * (r + 1)] for r in range(n_grp)]
        rank = [jnp.zeros((8, T), jnp.int32) for _ in range(n_grp)]
        for ii in range(NB):
            si = score_scr[g, pl.ds(ii, 1), :]
            r0 = ii // 8
            for r in range(n_grp):
                ge = jnp.where(si >= grp[r], 1, 0)
                gt = jnp.where(si > grp[r], 1, 0)
                if r > r0:
                    beats = ge
                elif r < r0:
                    beats = gt
                else:
                    beats = jnp.where(row8 + 8 * r > ii, ge, gt)
                rank[r] = rank[r] + beats
        sel = (jnp.concatenate(rank, axis=0) < nsel) & valid
        selb = jnp.where(sel, 0.0, NEG)
        if NB < LANES:
            selb = jnp.concatenate([selb, jnp.full((LANES - NB, T), NEG, F32)], axis=0)
        selb = selb.astype(BF16)
        q_sel[g] = jnp.concatenate([qt4[g], jnp.concatenate([selb] * GQA, axis=1)], axis=0)

    def finish_window(g, s):
        vt = jnp.concatenate([vwt_ref[0, g, w0 // WIN_CHUNK + j] for j in range(win_keys // WIN_CHUNK)], axis=1)
        win_state[g] = _softmax_step(win_state[g], s, vt, win_mask, GQA)

    units = [("cmp", g) for g in range(N_KV)] + [("win", g) for g in range(N_KV)]
    lookahead = 2
    scores = []
    for u, (kind, g) in enumerate(units):
        while len(scores) < min(len(units), u + 1 + lookahead):
            kind_n, g_n = units[len(scores)]
            keys = kc_ref[0, g_n] if kind_n == "cmp" else kw_ref[0, g_n, pl.ds(w0, win_keys), :]
            scores.append(_dot(keys, qt4[g_n]))
        if kind == "cmp":
            finish_compressed(g, scores[u])
        else:
            finish_window(g, scores[u])

    for g in range(N_KV):
        qsel_scr[g] = q_sel[g]

    def sel_scores(slot, g, c):
        s_scr[slot, g] = _dot(key_rows(ks_ref, g, c), qsel_scr[g])

    last_chunk = ks_ref.shape[2] // KC - 1

    def sel_body(j, states):
        states = list(states)
        for slot in range(2):
            c = 2 * j + slot
            for g in range(N_KV):
                sel_scores(1 - slot, g, c + 1)
            for g in range(N_KV):
                states[g] = _softmax_step(states[g], s_scr.at[slot, g], vst_ref[0, g, c], None, GQA)
        return tuple(states)

    for g in range(N_KV):
        sel_scores(0, g, 0)
    n_pairs = diag // 2
    sel_state = list(lax.fori_loop(0, n_pairs, sel_body, tuple(init_state() for _ in range(N_KV))))
    def tail_step(states, slot):
        c = 2 * n_pairs + slot
        c_load = jnp.minimum(c, last_chunk)
        causal = (rel + (t0 - c * KC), 1 << 30)
        return tuple(_softmax_step(states[g], s_scr.at[slot, g], vst_ref[0, g, c_load], causal, GQA)
                     for g in range(N_KV))

    for g in range(N_KV):
        sel_scores(1, g, jnp.minimum(2 * n_pairs + 1, last_chunk))
    sel_state = tail_step(tuple(sel_state), 0)
    sel_state = lax.cond(2 * n_pairs + 1 <= diag, lambda st: tail_step(st, 1), lambda st: st, sel_state)
    os_t, ow_t = [], []
    for g in range(N_KV):
        acc = sel_state[g][1]
        os_t.append(acc[0:HEAD_DIM] / jnp.maximum(acc[HEAD_DIM:HEAD_DIM + 1], TINY))
        acc = win_state[g][1]
        ow_t.append(acc[0:HEAD_DIM] / jnp.maximum(acc[HEAD_DIM:HEAD_DIM + 1], TINY))

    gt = gt_ref[0]
    ogt = ogt_ref[...]
    pieces = []
    for hh in range(N_HEADS):
        g, h = divmod(hh, GQA)
        sl = slice(T * h, T * (h + 1))
        o = (gt[hh:hh + 1] * oc_t[g][0:HEAD_DIM, sl] + gt[N_HEADS + hh:N_HEADS + hh + 1] * os_t[g][:, sl]
             + gt[2 * N_HEADS + hh:2 * N_HEADS + hh + 1] * ow_t[g][:, sl])
        msq = jnp.mean(o * o, axis=0, keepdims=True)
        pieces.append(o * lax.rsqrt(msq + EPS) * ogt[:, hh:hh + 1])
    o_ref[0] = jnp.concatenate(pieces, axis=0).T.astype(BF16)


def _attention(qt, gatet, kc, vct, ks, vst, kw, vwt, ovl, ogt):
    bsz, _, _, s = qt.shape
    T = Q_TILE
    ncp = kc.shape[2]
    nb = ovl.shape[0]
    full4 = lambda a: pl.BlockSpec((1,) + a.shape[1:], lambda b, i: (b,) + (0,) * (a.ndim - 1))
    return pl.pallas_call(
        _attn_kernel,
        out_shape=jax.ShapeDtypeStruct((bsz, s, ATTN_WIDTH), BF16),
        grid=(bsz, s // T),
        in_specs=[pl.BlockSpec((1, N_HEADS, LANES, T), lambda b, i: (b, 0, 0, i)),
                  pl.BlockSpec((1, GATE_ROWS, T), lambda b, i: (b, 0, i)),
                  full4(kc), full4(vct), full4(ks), full4(vst), full4(kw), full4(vwt),
                  pl.BlockSpec(ovl.shape, lambda b, i: (0, 0)),
                  pl.BlockSpec(ogt.shape, lambda b, i: (0, 0))],
        out_specs=pl.BlockSpec((1, T, ATTN_WIDTH), lambda b, i: (b, i, 0)),
        scratch_shapes=[pltpu.VMEM((N_KV, nb, T), F32),
                        pltpu.VMEM((N_KV, K_SEL_WIDTH, GQA * T), BF16),
                        pltpu.VMEM((2, N_KV, K_CHUNK, GQA * T), F32)],
        compiler_params=pltpu.CompilerParams(dimension_semantics=("arbitrary", "arbitrary"),
                                             vmem_limit_bytes=VMEM_LIMIT),
        name="nsa_attn",
    )(qt, gatet, kc, vct, ks, vst, kw, vwt, ovl, ogt)


def _ffn_kernel(x_ref, a_ref, cv_ref, mod_ref, n2_ref, wo_ref, wg_ref, wu_ref, wd_ref, cw_ref,
                o_ref, h2_scr, gp_scr, act_scr):
    i = pl.program_id(1)
    tm = x_ref.shape[1]
    n_chunks = wg_ref.shape[1] // FF_CHUNK
    mod = mod_ref[0]
    mix = _dot(a_ref[0], wo_ref[0:ATTN_WIDTH, :]) + _dot(cv_ref[0], wo_ref[ATTN_WIDTH:ATTN_WIDTH + CONV_WIDTH, :])
    x1 = x_ref[0] + mod[2:3] * mix
    o_ref[0] = x1
    ms = jnp.mean(x1 * x1, axis=-1, keepdims=True)
    h2 = (x1 * lax.rsqrt(ms + EPS) * n2_ref[...]) * (1.0 + mod[4:5]) + mod[3:4]

    @pl.when(i == 0)
    def _():
        h2_scr[0:HALO, :] = jnp.zeros((HALO, h2_scr.shape[1]), BF16)

    h2_scr[HALO:HALO + tm, :] = h2.astype(BF16)

    for c in range(n_chunks):
        slot = c % 2
        cols = slice(FF_CHUNK * c, FF_CHUNK * (c + 1))
        gp_scr[slot] = _dot(h2_scr[...], wg_ref[:, cols])
        cw = cw_ref[:, cols]
        g_pre = (cw[0:1] * gp_scr[slot, pl.ds(HALO - 2, tm), :] + cw[1:2] * gp_scr[slot, pl.ds(HALO - 1, tm), :]
                 + cw[2:3] * gp_scr[slot, pl.ds(HALO, tm), :])
        up = _dot(h2_scr[pl.ds(HALO, tm), :], wu_ref[:, cols])
        act_scr[:, cols] = (jax.nn.silu(g_pre) * up).astype(BF16)

    o_ref[0] = o_ref[0] + mod[5:6] * _dot(act_scr[...], wd_ref[...])
    h2_scr[0:HALO, :] = h2_scr[tm:tm + HALO, :]


def _out_ffn(x, attn_n, conv_n, mod3, n2, wo, wg3, wu3, wd, cw3):
    bsz, s, d = x.shape
    tm = TM_FFN
    resident = lambda shape: pl.BlockSpec(shape, lambda b, i: (0,) * len(shape), pipeline_mode=pl.Buffered(1))
    tok = lambda w: pl.BlockSpec((1, tm, w), lambda b, i: (b, i, 0))
    return pl.pallas_call(
        _ffn_kernel,
        out_shape=jax.ShapeDtypeStruct((bsz, s, d), F32),
        grid=(bsz, s // tm),
        in_specs=[tok(d), tok(ATTN_WIDTH), tok(CONV_WIDTH),
                  pl.BlockSpec((1, 6, d), lambda b, i: (b, 0, 0)),
                  resident(n2.shape), resident(wo.shape), resident(wg3.shape), resident(wu3.shape),
                  resident(wd.shape), resident(cw3.shape)],
        out_specs=tok(d),
        scratch_shapes=[pltpu.VMEM((tm + HALO, d), BF16),
                        pltpu.VMEM((2, tm + HALO, FF_CHUNK), F32),
                        pltpu.VMEM((tm, wd.shape[0]), BF16)],
        compiler_params=pltpu.CompilerParams(dimension_semantics=("arbitrary", "arbitrary"),
                                             vmem_limit_bytes=VMEM_LIMIT),
        name="out_ffn",
    )(x, attn_n, conv_n, mod3, n2, wo, wg3, wu3, wd, cw3)


def _pos_feats(pos):
    return np.stack([pos // SEL_BLOCK, pos % SEL_BLOCK] * SLOPE_TERMS, axis=-1)


def _bf16_round(v):
    bits = np.asarray(v, np.float32).view(np.uint32)
    bits = (bits + np.uint32(0x7FFF) + ((bits >> np.uint32(16)) & np.uint32(1))) & np.uint32(0xFFFF0000)
    return bits.view(np.float32)


def _static_tables(s):
    ncp = s // CMP_STRIDE
    n_cmp = (s - CMP_LEN) // CMP_STRIDE + 1
    nb = s // SEL_BLOCK
    cs = np.arange(ncp) * CMP_STRIDE
    bs = np.arange(nb) * SEL_BLOCK
    ovl = ((cs[None, :] < bs[:, None] + SEL_BLOCK) & (cs[None, :] + CMP_LEN > bs[:, None])
           & (np.arange(ncp)[None, :] < n_cmp)).astype(np.float32)
    gmat = (np.arange(MXU_TILE)[:, None] // HEAD_DIM == np.arange(MXU_TILE)[None, :] // HEAD_DIM)

    t = np.arange(s)
    qfeat = np.zeros((N_HEADS, HEAD_DIM), np.float32)
    for hh in range(N_HEADS):
        rest = np.float32(2.0 ** (-8.0 * (hh + 1) / N_HEADS) * LOG2E)
        for i in range(SLOPE_TERMS):
            term = _bf16_round(rest)
            rest = np.float32(rest - term)
            qfeat[hh, 2 * i] = term * SEL_BLOCK
            qfeat[hh, 2 * i + 1] = term
    ktab = np.zeros((s, K_SEL_WIDTH - HEAD_DIM), np.float32)
    ktab[:, 0:POS_FEATS] = _pos_feats(t)
    ktab[t, HEAD_DIM + t // SEL_BLOCK] = 1.0
    ctab = np.zeros((ncp, LANES - HEAD_DIM), np.float32)
    ctab[:, 0:POS_FEATS] = _pos_feats(cs + CMP_LEN - 1)
    as_bf16 = lambda a: jnp.asarray(a, BF16)
    return (as_bf16(ovl), as_bf16(gmat.astype(np.float32)), jnp.asarray(qfeat.reshape(1, ATTN_WIDTH)),
            as_bf16(ktab), as_bf16(ctab))


def kernel(x, c, w_ada, b_ada, norm1_g, w_in, b_gate, q_norm_g, k_norm_cmp_g, k_norm_slc_g, k_norm_win_g,
           pos_cmp_k, pos_cmp_v, w_cmp_k1, w_cmp_k2, w_cmp_v1, w_cmp_v2, conv_mix_w, attn_out_g, conv_out_g,
           w_out, norm2_g, w_ffn_gate, w_ffn_up, conv_ffn_w, w_ffn_down):
    bsz, s, d = x.shape
    assert w_ada.shape[0] == 1, "single layer"
    assert s % TM_PROJ == 0 and s % TM_FFN == 0 and s % K_CHUNK == 0 and s % Q_TILE == 0
    assert s // SEL_BLOCK <= LANES and (s // SEL_BLOCK) % 8 == 0 and s >= WINDOW + Q_TILE
    d_ff = w_ffn_gate.shape[-1]
    assert d_ff % FF_CHUNK == 0
    n_ch = d_ff // FF_CHUNK
    ovl, gmat, qfeat, ktab, ctab = _static_tables(s)

    mod = _ada(c, w_ada[0], b_ada[0][None, :])
    mod3 = mod.reshape(bsz, 6, d)

    w = w_in[0]
    o_q, o_kv, o_gl = ATTN_WIDTH, ATTN_WIDTH + 6 * KV_COLS, ATTN_WIDTH + 6 * KV_COLS + N_GATE
    wq = w[:, :o_q].astype(BF16)
    wkv = w[:, o_q:o_kv].astype(BF16)
    perm = np.array([kv * GQA * 3 + h * 3 + br for br in range(3) for kv in range(N_KV) for h in range(GQA)])
    wgl = jnp.pad(w[:, o_kv:o_gl][:, perm], ((0, 0), (0, LANES - N_GATE))).astype(BF16)
    bg = jnp.pad(b_gate[0][perm], (0, LANES - N_GATE))[None, :]
    wc = w[:, o_gl:].astype(BF16)
    qg = jnp.tile(q_norm_g[0], GROUPS_PER_MXU_TILE)[None, :]
    kg = jnp.concatenate([jnp.tile(k_norm_slc_g[0], N_KV), jnp.tile(k_norm_win_g[0], N_KV)])[None, :]

    qt, kvc_raw, ks, vst, kw, vwt, gatet, conv_n = _in_proj(
        x, mod3, norm1_g, wq, wkv, wgl, wc, bg, qg, kg, conv_mix_w[0], conv_out_g, gmat, qfeat, ktab)

    half = CMP_STRIDE * HEAD_DIM
    pos2 = jnp.stack([pos_cmp_k[0].reshape(2, half), pos_cmp_v[0].reshape(2, half)])
    w1 = jnp.stack([w_cmp_k1[0], w_cmp_v1[0]]).reshape(2, 2, half, -1).astype(BF16)
    w2 = jnp.stack([w_cmp_k2[0], w_cmp_v2[0]]).astype(BF16)
    kc, vct = _compress(kvc_raw, pos2, w1, w2, k_norm_cmp_g, ctab)

    attn_n = _attention(qt, gatet, kc, vct, ks, vst, kw, vwt, ovl, attn_out_g[0].reshape(N_HEADS, HEAD_DIM).T)

    wo = w_out[0].astype(BF16)
    wg3 = w_ffn_gate[0].astype(BF16)
    wu3 = w_ffn_up[0].astype(BF16)
    wd = w_ffn_down[0].astype(BF16)
    cw3 = conv_ffn_w[0]
    return _out_ffn(x, attn_n, conv_n, mod3, norm2_g, wo, wg3, wu3, wd, cw3)
```

```python
import numpy as np
import jax
import jax.numpy as jnp
from jax import lax
from jax.experimental import pallas as pl
from jax.experimental.pallas import tpu as pltpu

F32 = jnp.float32
BF16 = jnp.bfloat16

N_HEADS = 8
N_KV = 2
GQA = N_HEADS // N_KV
HEAD_DIM = 64
ATTN_WIDTH = N_HEADS * HEAD_DIM
KV_COLS = N_KV * HEAD_DIM
N_GATE = 3 * N_HEADS
CMP_LEN = 32
CMP_STRIDE = 16
SEL_BLOCK = 64
SEL_TOPK = 16
WINDOW = 512
FORCE_SCORE = 1.0e4
CONV_WIDTH = 512
EPS = 1e-6
NEG = -1.0e30
TINY = float(np.finfo(np.float32).tiny)

LANES = 128
BF16_SUBLANES = 16
GROUPS_PER_MXU_TILE = 4
MXU_TILE = GROUPS_PER_MXU_TILE * HEAD_DIM
VMEM_LIMIT = 56 * 1024 * 1024

TM_PROJ = 1024
PROJ_SPLIT = 4
TM_FFN = 1024
FF_CHUNK = 256
Q_TILE = 128
K_CHUNK = 256
WIN_CHUNK = 128
HALO = 16

LOG2E = float(np.log2(np.e))
SLOPE_TERMS = 3
POS_FEATS = 2 * SLOPE_TERMS
GATE_ROWS = 32
V_ROWS = HEAD_DIM + BF16_SUBLANES
K_SEL_WIDTH = 2 * LANES


def _dot(a, b):
    return jnp.dot(a, b, preferred_element_type=F32)


def _group_rms(v, gmat, gain):
    ssq = _dot((v * v).astype(BF16), gmat)
    return v * lax.rsqrt(ssq * (1.0 / HEAD_DIM) + EPS) * gain


def _ada_kernel(c_ref, w_ref, b_ref, o_ref):
    a = jax.nn.silu(c_ref[...])
    o_ref[...] = jnp.dot(a, w_ref[...], preferred_element_type=F32,
                         precision=lax.Precision.HIGHEST) + b_ref[...]


def _ada(c, w, b):
    bsz, d = c.shape
    n = w.shape[1]
    return pl.pallas_call(
        _ada_kernel,
        out_shape=jax.ShapeDtypeStruct((bsz, n), F32),
        grid=(n // d,),
        in_specs=[pl.BlockSpec((bsz, d), lambda j: (0, 0)),
                  pl.BlockSpec((d, d), lambda j: (0, j)),
                  pl.BlockSpec((1, d), lambda j: (0, j))],
        out_specs=pl.BlockSpec((bsz, d), lambda j: (0, j)),
        compiler_params=pltpu.CompilerParams(dimension_semantics=("arbitrary",),
                                             vmem_limit_bytes=VMEM_LIMIT),
        name="ada",
    )(c, w, b)


def _inproj_kernel(x_ref, mod_ref, n1_ref, wq_ref, wkv_ref, wgl_ref, wc_ref, bg_ref, qg_ref, kg_ref,
                   cw_ref, cg_ref, gmat_ref, qfeat_ref, ktab_ref,
                   qt_out, kvc_out, ks_out, vst_out, kw_out, vwt_out, gatet_out, conv_out, u_scr):
    i = pl.program_id(1)
    tm = x_ref.shape[1]
    th = tm // PROJ_SPLIT
    mod = mod_ref[0]
    gmat = gmat_ref[...]
    cw = cw_ref[...]

    @pl.when(i == 0)
    def _():
        u_scr[0:HALO, :] = jnp.zeros((HALO, CONV_WIDTH), F32)

    proj = []
    for r0 in range(0, tm, th):
        x = x_ref[0, r0:r0 + th, :]
        ms = jnp.mean(x * x, axis=-1, keepdims=True)
        y = x * lax.rsqrt(ms + EPS) * n1_ref[...]
        hb = (y * (1.0 + mod[1:2]) + mod[0:1]).astype(BF16)
        proj.append((_dot(hb, wq_ref[...]), _dot(hb, wkv_ref[...]), _dot(hb, wgl_ref[...]), _dot(hb, wc_ref[...])))

    qfeat = jnp.broadcast_to(qfeat_ref[...], (th, ATTN_WIDTH))
    for (pq, pkv, pg, pc), r0 in zip(proj, range(0, tm, th)):
        rows = slice(r0, r0 + th)
        for j in range(ATTN_WIDTH // MXU_TILE):
            qn = _group_rms(pq[:, MXU_TILE * j:MXU_TILE * (j + 1)], gmat, qg_ref[...]) * (HEAD_DIM ** -0.5 * LOG2E)
            for hl in range(GROUPS_PER_MXU_TILE):
                hh = GROUPS_PER_MXU_TILE * j + hl
                qa = jnp.concatenate([qn[:, HEAD_DIM * hl:HEAD_DIM * (hl + 1)],
                                      qfeat[:, HEAD_DIM * hh:HEAD_DIM * (hh + 1)]], axis=-1)
                qt_out[0, hh, :, rows] = qa.T.astype(BF16)

        for j in range(2 * N_KV):
            kvc_out[0, j, rows, :] = pkv[:, HEAD_DIM * j:HEAD_DIM * (j + 1)]
        ksw = jnp.concatenate([pkv[:, 2 * KV_COLS:3 * KV_COLS], pkv[:, 4 * KV_COLS:5 * KV_COLS]], axis=-1)
        kswn = _group_rms(ksw, gmat, kg_ref[...])
        ktab = ktab_ref[rows, :].astype(F32)
        vs_t = pkv[:, 3 * KV_COLS:4 * KV_COLS].T.astype(BF16)
        vw_t = pkv[:, 5 * KV_COLS:6 * KV_COLS].T.astype(BF16)
        for g in range(N_KV):
            ks_out[0, g, rows, :] = jnp.concatenate(
                [kswn[:, HEAD_DIM * g:HEAD_DIM * (g + 1)], ktab], axis=-1).astype(BF16)
            kw_out[0, g, rows, :] = jnp.concatenate(
                [kswn[:, KV_COLS + HEAD_DIM * g:KV_COLS + HEAD_DIM * (g + 1)], ktab[:, 0:HEAD_DIM]],
                axis=-1).astype(BF16)
            for out, v_t, kc_n in ((vst_out, vs_t, K_CHUNK), (vwt_out, vw_t, WIN_CHUNK)):
                ones_rows = jnp.where(lax.broadcasted_iota(jnp.int32, (BF16_SUBLANES, kc_n), 0) == 0,
                                      1.0, 0.0).astype(BF16)
                for j in range(th // kc_n):
                    cj = r0 // kc_n + j
                    out[0, g, cj, 0:HEAD_DIM, :] = v_t[HEAD_DIM * g:HEAD_DIM * (g + 1), kc_n * j:kc_n * (j + 1)]
                    out[0, g, cj, HEAD_DIM:V_ROWS, :] = ones_rows

        gate = jax.nn.sigmoid(pg + bg_ref[...])
        gatet_out[0, :, rows] = gate.T[0:GATE_ROWS]

        gate_b = pc[:, 0:CONV_WIDTH]
        u = pc[:, CONV_WIDTH:2 * CONV_WIDTH] * pc[:, 2 * CONV_WIDTH:3 * CONV_WIDTH]
        u_scr[HALO + r0:HALO + r0 + th, :] = u
        conv = gate_b * (cw[0:1] * u_scr[pl.ds(HALO + r0 - 2, th), :] + cw[1:2] * u_scr[pl.ds(HALO + r0 - 1, th), :]
                         + cw[2:3] * u)
        for j in range(CONV_WIDTH // MXU_TILE):
            sl = slice(MXU_TILE * j, MXU_TILE * (j + 1))
            conv_out[0, rows, sl] = _group_rms(conv[:, sl], gmat, cg_ref[:, sl]).astype(BF16)
    u_scr[0:HALO, :] = u_scr[tm:tm + HALO, :]


def _in_proj(x, mod3, n1, wq, wkv, wgl, wc, bg, qg, kg, cw, cg, gmat, qfeat, ktab):
    bsz, s, d = x.shape
    tm = TM_PROJ
    const = lambda shape: pl.BlockSpec(shape, lambda b, i: (0,) * len(shape))
    tok = lambda w: pl.BlockSpec((1, tm, w), lambda b, i: (b, i, 0))
    tab = lambda w: pl.BlockSpec((tm, w), lambda b, i: (i, 0))
    kspec = lambda w: pl.BlockSpec((1, N_KV, tm, w), lambda b, i: (b, 0, i, 0))
    vtspec = lambda kc: pl.BlockSpec((1, N_KV, tm // kc, V_ROWS, kc), lambda b, i: (b, 0, i, 0, 0))
    vt_shape = lambda kc: jax.ShapeDtypeStruct((bsz, N_KV, s // kc, V_ROWS, kc), BF16)
    return pl.pallas_call(
        _inproj_kernel,
        out_shape=(jax.ShapeDtypeStruct((bsz, N_HEADS, LANES, s), BF16),
                   jax.ShapeDtypeStruct((bsz, 2 * N_KV, s, HEAD_DIM), F32),
                   jax.ShapeDtypeStruct((bsz, N_KV, s, K_SEL_WIDTH), BF16), vt_shape(K_CHUNK),
                   jax.ShapeDtypeStruct((bsz, N_KV, s, LANES), BF16), vt_shape(WIN_CHUNK),
                   jax.ShapeDtypeStruct((bsz, GATE_ROWS, s), F32),
                   jax.ShapeDtypeStruct((bsz, s, CONV_WIDTH), BF16)),
        grid=(bsz, s // tm),
        in_specs=[tok(d),
                  pl.BlockSpec((1, 6, d), lambda b, i: (b, 0, 0)),
                  const(n1.shape), const(wq.shape), const(wkv.shape), const(wgl.shape), const(wc.shape),
                  const(bg.shape), const(qg.shape), const(kg.shape), const(cw.shape), const(cg.shape),
                  const(gmat.shape), const(qfeat.shape), tab(ktab.shape[1])],
        out_specs=(pl.BlockSpec((1, N_HEADS, LANES, tm), lambda b, i: (b, 0, 0, i)),
                   pl.BlockSpec((1, 2 * N_KV, tm, HEAD_DIM), lambda b, i: (b, 0, i, 0)),
                   kspec(K_SEL_WIDTH), vtspec(K_CHUNK), kspec(LANES), vtspec(WIN_CHUNK),
                   pl.BlockSpec((1, GATE_ROWS, tm), lambda b, i: (b, 0, i)),
                   tok(CONV_WIDTH)),
        scratch_shapes=[pltpu.VMEM((tm + HALO, CONV_WIDTH), F32)],
        compiler_params=pltpu.CompilerParams(dimension_semantics=("arbitrary", "arbitrary"),
                                             vmem_limit_bytes=VMEM_LIMIT),
        name="in_proj",
    )(x, mod3, n1, wq, wkv, wgl, wc, bg, qg, kg, cw, cg, gmat, qfeat, ktab)


def _cmp_kernel(kraw_ref, vraw_ref, pos_ref, w1_ref, w2_ref, g_ref, ctab_ref, kc_out, vct_out):
    ncp = kraw_ref.shape[2] // CMP_STRIDE

    def mlp(j):
        raw_ref = (kraw_ref, vraw_ref)[j]
        a = jnp.concatenate([raw_ref[0, 0, pl.ds(l, ncp, stride=CMP_STRIDE), :] for l in range(CMP_STRIDE)],
                            axis=-1)
        pos = pos_ref[j]
        ha = _dot((a + pos[0:1]).astype(BF16), w1_ref[j, 0])
        hb = _dot((a + pos[1:2]).astype(BF16), w1_ref[j, 1])
        h = ha + pltpu.roll(hb, ncp - 1, axis=0)
        return _dot(jax.nn.gelu(h).astype(BF16), w2_ref[j])

    ck = mlp(0)
    ckn = ck * lax.rsqrt(jnp.mean(ck * ck, axis=-1, keepdims=True) + EPS) * g_ref[...]
    kc_out[0, 0] = jnp.concatenate([ckn.astype(BF16), ctab_ref[...]], axis=-1)
    cv = mlp(1)
    cv_t = jnp.concatenate([cv, jnp.zeros((ncp, LANES - HEAD_DIM), F32)], axis=-1).T
    vct_out[0, 0, 0:HEAD_DIM, :] = cv_t[0:HEAD_DIM].astype(BF16)
    vct_out[0, 0, HEAD_DIM:V_ROWS, :] = jnp.where(
        lax.broadcasted_iota(jnp.int32, (BF16_SUBLANES, ncp), 0) == 0, 1.0, 0.0).astype(BF16)


def _compress(kvc_raw, pos2, w1, w2, g, ctab):
    bsz, _, s, _ = kvc_raw.shape
    ncp = s // CMP_STRIDE
    full = lambda a: pl.BlockSpec(a.shape, lambda b, j: (0,) * a.ndim)
    return pl.pallas_call(
        _cmp_kernel,
        out_shape=(jax.ShapeDtypeStruct((bsz, N_KV, ncp, LANES), BF16),
                   jax.ShapeDtypeStruct((bsz, N_KV, V_ROWS, ncp), BF16)),
        grid=(bsz, N_KV),
        in_specs=[pl.BlockSpec((1, 1, s, HEAD_DIM), lambda b, j: (b, j, 0, 0)),
                  pl.BlockSpec((1, 1, s, HEAD_DIM), lambda b, j: (b, N_KV + j, 0, 0)),
                  full(pos2), full(w1), full(w2), full(g), full(ctab)],
        out_specs=(pl.BlockSpec((1, 1, ncp, LANES), lambda b, j: (b, j, 0, 0)),
                   pl.BlockSpec((1, 1, V_ROWS, ncp), lambda b, j: (b, j, 0, 0))),
        compiler_params=pltpu.CompilerParams(dimension_semantics=("arbitrary", "arbitrary"),
                                             vmem_limit_bytes=VMEM_LIMIT),
        name="compress",
    )(kvc_raw, kvc_raw, pos2, w1, w2, g, ctab)


SOFTMAX_SLAB = 64


def _softmax_step(state, s, vt, mask, n_rep):
    m, acc = state
    kc = s.shape[0]
    if mask is not None:
        dist, hi = mask
        t = dist.shape[1]
        keep = {r: (dist[r:r + SOFTMAX_SLAB] >= 0) & (dist[r:r + SOFTMAX_SLAB] < hi)
                for r in range(0, kc, SOFTMAX_SLAB)}

    def slab(r):
        blk = s[r:r + SOFTMAX_SLAB, :]
        if mask is None:
            return blk
        return jnp.concatenate([jnp.where(keep[r], blk[:, t * h:t * (h + 1)], NEG) for h in range(n_rep)], axis=1)

    m_new = m
    for r in range(0, kc, SOFTMAX_SLAB):
        m_new = jnp.maximum(m_new, jnp.max(slab(r), axis=0, keepdims=True))
    p = jnp.concatenate([jnp.exp2(slab(r) - m_new).astype(BF16) for r in range(0, kc, SOFTMAX_SLAB)], axis=0)
    alpha = jnp.exp2(m - m_new)
    return m_new, alpha * acc + _dot(vt, p)


def _attn_kernel(qt_ref, gt_ref, kc_ref, vct_ref, ks_ref, vst_ref, kw_ref, vwt_ref, ovl_ref, ogt_ref,
                 o_ref, score_scr, qsel_scr, s_scr):
    T = qt_ref.shape[3]
    KC = K_CHUNK
    NB, NCP = ovl_ref.shape
    nsel = min(SEL_TOPK, NB)
    i = pl.program_id(1)
    t0 = i * T
    diag = (t0 + T - 1) // KC

    tcol = lax.broadcasted_iota(jnp.int32, (KC, T), 1)
    krow = lax.broadcasted_iota(jnp.int32, (KC, T), 0)
    rel = tcol - krow

    cend = CMP_STRIDE * lax.broadcasted_iota(jnp.int32, (NCP, T), 0) + (CMP_LEN - 1)
    mask_c = cend <= t0 + lax.broadcasted_iota(jnp.int32, (NCP, T), 1)

    blk = lax.broadcasted_iota(jnp.int32, (NB, T), 0)
    cur = (t0 + lax.broadcasted_iota(jnp.int32, (NB, T), 1)) // SEL_BLOCK
    valid = blk <= cur
    forced = (blk == 0) | (blk == cur) | (blk == cur - 1)
    row8 = lax.broadcasted_iota(jnp.int32, (8, T), 0)

    win_keys = WINDOW + T
    w0 = pl.multiple_of(jnp.maximum(t0 - WINDOW, 0), WIN_CHUNK)
    win_mask = ((t0 - w0) + lax.broadcasted_iota(jnp.int32, (win_keys, T), 1)
                - lax.broadcasted_iota(jnp.int32, (win_keys, T), 0), WINDOW)

    def init_state():
        return (jnp.full((1, GQA * T), 2.0 * NEG, F32), jnp.zeros((V_ROWS, GQA * T), F32))

    def key_rows(k_ref, g, c):
        return k_ref[0, g, pl.ds(pl.multiple_of(c * KC, KC), KC), :]

    qt4 = [jnp.concatenate([qt_ref[0, GQA * g + h] for h in range(GQA)], axis=1) for g in range(N_KV)]
    oc_t, q_sel = [None] * N_KV, [None] * N_KV
    win_state = [init_state() for _ in range(N_KV)]

    def finish_compressed(g, s):
        ps = []
        for h in range(GQA):
            sh = jnp.where(mask_c, s[:, T * h:T * (h + 1)], NEG)
            m = jnp.max(sh, axis=0, keepdims=True)
            m = jnp.where(m > 0.5 * NEG, m, 0.0)
            e = jnp.exp2(sh - m)
            ps.append(e / jnp.maximum(jnp.sum(e, axis=0, keepdims=True), TINY))
        oc_t[g] = _dot(vct_ref[0, g], jnp.concatenate(ps, axis=1).astype(BF16))

        psum = ((ps[0] + ps[1]) + ps[2]) + ps[3]
        p_hi = psum.astype(BF16)
        p_lo = (psum - p_hi.astype(F32)).astype(BF16)
        ovl = ovl_ref[...]
        imp_t = _dot(ovl, p_hi) + _dot(ovl, p_lo)
        score = jnp.where(forced, FORCE_SCORE, jnp.where(valid, imp_t, -1.0))
        score_scr[g] = score
        n_grp = NB // 8
        grp = [score[8 * r:8 * (r + 1)] for r in range(n_grp)]
        rank = [jnp.zeros((8, T), jnp.int32) for _ in range(n_grp)]
        for ii in range(NB):
            si = score_scr[g, pl.ds(ii, 1), :]
            r0 = ii // 8
            for r in range(n_grp):
                ge = jnp.where(si >= grp[r], 1, 0)
                gt = jnp.where(si > grp[r], 1, 0)
                if r > r0:
                    beats = ge
                elif r < r0:
                    beats = gt
                else:
                    beats = jnp.where(row8 + 8 * r > ii, ge, gt)
                rank[r] = rank[r] + beats
        sel = (jnp.concatenate(rank, axis=0) < nsel) & valid
        selb = jnp.where(sel, 0.0, NEG)
        if NB < LANES:
            selb = jnp.concatenate([selb, jnp.full((LANES - NB, T), NEG, F32)], axis=0)
        selb = selb.astype(BF16)
        q_sel[g] = jnp.concatenate([qt4[g], jnp.concatenate([selb] * GQA, axis=1)], axis=0)

    def finish_window(g, s):
        vt = jnp.concatenate([vwt_ref[0, g, w0 // WIN_CHUNK + j] for j in range(win_keys // WIN_CHUNK)], axis=1)
        win_state[g] = _softmax_step(win_state[g], s, vt, win_mask, GQA)

    units = [("cmp", g) for g in range(N_KV)] + [("win", g) for g in range(N_KV)]
    lookahead = 2
    scores = []
    for u, (kind, g) in enumerate(units):
        while len(scores) < min(len(units), u + 1 + lookahead):
            kind_n, g_n = units[len(scores)]
            keys = kc_ref[0, g_n] if kind_n == "cmp" else kw_ref[0, g_n, pl.ds(w0, win_keys), :]
            scores.append(_dot(keys, qt4[g_n]))
        if kind == "cmp":
            finish_compressed(g, scores[u])
        else:
            finish_window(g, scores[u])

    for g in range(N_KV):
        qsel_scr[g] = q_sel[g]

    def sel_scores(slot, g, c):
        s_scr[slot, g] = _dot(key_rows(ks_ref, g, c), qsel_scr[g])

    last_chunk = ks_ref.shape[2] // KC - 1

    def sel_body(j, states):
        states = list(states)
        for slot in range(2):
            c = 2 * j + slot
            for g in range(N_KV):
                sel_scores(1 - slot, g, c + 1)
            for g in range(N_KV):
                states[g] = _softmax_step(states[g], s_scr.at[slot, g], vst_ref[0, g, c], None, GQA)
        return tuple(states)

    for g in range(N_KV):
        sel_scores(0, g, 0)
    n_pairs = diag // 2
    sel_state = list(lax.fori_loop(0, n_pairs, sel_body, tuple(init_state() for _ in range(N_KV))))
    def tail_step(states, slot):
        c = 2 * n_pairs + slot
        c_load = jnp.minimum(c, last_chunk)
        causal = (rel + (t0 - c * KC), 1 << 30)
        return tuple(_softmax_step(states[g], s_scr.at[slot, g], vst_ref[0, g, c_load], causal, GQA)
                     for g in range(N_KV))

    for g in range(N_KV):
        sel_scores(1, g, jnp.minimum(2 * n_pairs + 1, last_chunk))
    sel_state = tail_step(tuple(sel_state), 0)
    sel_state = lax.cond(2 * n_pairs + 1 <= diag, lambda st: tail_step(st, 1), lambda st: st, sel_state)
    os_t, ow_t = [], []
    for g in range(N_KV):
        acc = sel_state[g][1]
        os_t.append(acc[0:HEAD_DIM] / jnp.maximum(acc[HEAD_DIM:HEAD_DIM + 1], TINY))
        acc = win_state[g][1]
        ow_t.append(acc[0:HEAD_DIM] / jnp.maximum(acc[HEAD_DIM:HEAD_DIM + 1], TINY))

    gt = gt_ref[0]
    ogt = ogt_ref[...]
    pieces = []
    for hh in range(N_HEADS):
        g, h = divmod(hh, GQA)
        sl = slice(T * h, T * (h + 1))
        o = (gt[hh:hh + 1] * oc_t[g][0:HEAD_DIM, sl] + gt[N_HEADS + hh:N_HEADS + hh + 1] * os_t[g][:, sl]
             + gt[2 * N_HEADS + hh:2 * N_HEADS + hh + 1] * ow_t[g][:, sl])
        msq = jnp.mean(o * o, axis=0, keepdims=True)
        pieces.append(o * lax.rsqrt(msq + EPS) * ogt[:, hh:hh + 1])
    o_ref[0] = jnp.concatenate(pieces, axis=0).T.astype(BF16)


def _attention(qt, gatet, kc, vct, ks, vst, kw, vwt, ovl, ogt):
    bsz, _, _, s = qt.shape
    T = Q_TILE
    ncp = kc.shape[2]
    nb = ovl.shape[0]
    full4 = lambda a: pl.BlockSpec((1,) + a.shape[1:], lambda b, i: (b,) + (0,) * (a.ndim - 1))
    return pl.pallas_call(
        _attn_kernel,
        out_shape=jax.ShapeDtypeStruct((bsz, s, ATTN_WIDTH), BF16),
        grid=(bsz, s // T),
        in_specs=[pl.BlockSpec((1, N_HEADS, LANES, T), lambda b, i: (b, 0, 0, i)),
                  pl.BlockSpec((1, GATE_ROWS, T), lambda b, i: (b, 0, i)),
                  full4(kc), full4(vct), full4(ks), full4(vst), full4(kw), full4(vwt),
                  pl.BlockSpec(ovl.shape, lambda b, i: (0, 0)),
                  pl.BlockSpec(ogt.shape, lambda b, i: (0, 0))],
        out_specs=pl.BlockSpec((1, T, ATTN_WIDTH), lambda b, i: (b, i, 0)),
        scratch_shapes=[pltpu.VMEM((N_KV, nb, T), F32),
                        pltpu.VMEM((N_KV, K_SEL_WIDTH, GQA * T), BF16),
                        pltpu.VMEM((2, N_KV, K_CHUNK, GQA * T), F32)],
        compiler_params=pltpu.CompilerParams(dimension_semantics=("arbitrary", "arbitrary"),
                                             vmem_limit_bytes=VMEM_LIMIT),
        name="nsa_attn",
    )(qt, gatet, kc, vct, ks, vst, kw, vwt, ovl, ogt)


def _ffn_kernel(x_ref, a_ref, cv_ref, mod_ref, n2_ref, wo_ref, wg_ref, wu_ref, wd_ref, cw_ref,
                o_ref, h2_scr, gp_scr, act_scr):
    i = pl.program_id(1)
    tm = x_ref.shape[1]
    n_chunks = wg_ref.shape[1] // FF_CHUNK
    mod = mod_ref[0]
    mix = _dot(a_ref[0], wo_ref[0:ATTN_WIDTH, :]) + _dot(cv_ref[0], wo_ref[ATTN_WIDTH:ATTN_WIDTH + CONV_WIDTH, :])
    x1 = x_ref[0] + mod[2:3] * mix
    o_ref[0] = x1
    ms = jnp.mean(x1 * x1, axis=-1, keepdims=True)
    h2 = (x1 * lax.rsqrt(ms + EPS) * n2_ref[...]) * (1.0 + mod[4:5]) + mod[3:4]

    @pl.when(i == 0)
    def _():
        h2_scr[0:HALO, :] = jnp.zeros((HALO, h2_scr.shape[1]), BF16)

    h2_scr[HALO:HALO + tm, :] = h2.astype(BF16)

    for c in range(n_chunks):
        slot = c % 2
        cols = slice(FF_CHUNK * c, FF_CHUNK * (c + 1))
        gp_scr[slot] = _dot(h2_scr[...], wg_ref[:, cols])
        cw = cw_ref[:, cols]
        g_pre = (cw[0:1] * gp_scr[slot, pl.ds(HALO - 2, tm), :] + cw[1:2] * gp_scr[slot, pl.ds(HALO - 1, tm), :]
                 + cw[2:3] * gp_scr[slot, pl.ds(HALO, tm), :])
        up = _dot(h2_scr[pl.ds(HALO, tm), :], wu_ref[:, cols])
        act_scr[:, cols] = (jax.nn.silu(g_pre) * up).astype(BF16)

    o_ref[0] = o_ref[0] + mod[5:6] * _dot(act_scr[...], wd_ref[...])
    h2_scr[0:HALO, :] = h2_scr[tm:tm + HALO, :]


def _out_ffn(x, attn_n, conv_n, mod3, n2, wo, wg3, wu3, wd, cw3):
    bsz, s, d = x.shape
    tm = TM_FFN
    resident = lambda shape: pl.BlockSpec(shape, lambda b, i: (0,) * len(shape), pipeline_mode=pl.Buffered(1))
    tok = lambda w: pl.BlockSpec((1, tm, w), lambda b, i: (b, i, 0))
    return pl.pallas_call(
        _ffn_kernel,
        out_shape=jax.ShapeDtypeStruct((bsz, s, d), F32),
        grid=(bsz, s // tm),
        in_specs=[tok(d), tok(ATTN_WIDTH), tok(CONV_WIDTH),
                  pl.BlockSpec((1, 6, d), lambda b, i: (b, 0, 0)),
                  resident(n2.shape), resident(wo.shape), resident(wg3.shape), resident(wu3.shape),
                  resident(wd.shape), resident(cw3.shape)],
        out_specs=tok(d),
        scratch_shapes=[pltpu.VMEM((tm + HALO, d), BF16),
                        pltpu.VMEM((2, tm + HALO, FF_CHUNK), F32),
                        pltpu.VMEM((tm, wd.shape[0]), BF16)],
        compiler_params=pltpu.CompilerParams(dimension_semantics=("arbitrary", "arbitrary"),
                                             vmem_limit_bytes=VMEM_LIMIT),
        name="out_ffn",
    )(x, attn_n, conv_n, mod3, n2, wo, wg3, wu3, wd, cw3)


def _pos_feats(pos):
    return np.stack([pos // SEL_BLOCK, pos % SEL_BLOCK] * SLOPE_TERMS, axis=-1)


def _bf16_round(v):
    bits = np.asarray(v, np.float32).view(np.uint32)
    bits = (bits + np.uint32(0x7FFF) + ((bits >> np.uint32(16)) & np.uint32(1))) & np.uint32(0xFFFF0000)
    return bits.view(np.float32)


def _static_tables(s):
    ncp = s // CMP_STRIDE
    n_cmp = (s - CMP_LEN) // CMP_STRIDE + 1
    nb = s // SEL_BLOCK
    cs = np.arange(ncp) * CMP_STRIDE
    bs = np.arange(nb) * SEL_BLOCK
    ovl = ((cs[None, :] < bs[:, None] + SEL_BLOCK) & (cs[None, :] + CMP_LEN > bs[:, None])
           & (np.arange(ncp)[None, :] < n_cmp)).astype(np.float32)
    gmat = (np.arange(MXU_TILE)[:, None] // HEAD_DIM == np.arange(MXU_TILE)[None, :] // HEAD_DIM)

    t = np.arange(s)
    qfeat = np.zeros((N_HEADS, HEAD_DIM), np.float32)
    for hh in range(N_HEADS):
        rest = np.float32(2.0 ** (-8.0 * (hh + 1) / N_HEADS) * LOG2E)
        for i in range(SLOPE_TERMS):
            term = _bf16_round(rest)
            rest = np.float32(rest - term)
            qfeat[hh, 2 * i] = term * SEL_BLOCK
            qfeat[hh, 2 * i + 1] = term
    ktab = np.zeros((s, K_SEL_WIDTH - HEAD_DIM), np.float32)
    ktab[:, 0:POS_FEATS] = _pos_feats(t)
    ktab[t, HEAD_DIM + t // SEL_BLOCK] = 1.0
    ctab = np.zeros((ncp, LANES - HEAD_DIM), np.float32)
    ctab[:, 0:POS_FEATS] = _pos_feats(cs + CMP_LEN - 1)
    as_bf16 = lambda a: jnp.asarray(a, BF16)
    return (as_bf16(ovl), as_bf16(gmat.astype(np.float32)), jnp.asarray(qfeat.reshape(1, ATTN_WIDTH)),
            as_bf16(ktab), as_bf16(ctab))


def kernel(x, c, w_ada, b_ada, norm1_g, w_in, b_gate, q_norm_g, k_norm_cmp_g, k_norm_slc_g, k_norm_win_g,
           pos_cmp_k, pos_cmp_v, w_cmp_k1, w_cmp_k2, w_cmp_v1, w_cmp_v2, conv_mix_w, attn_out_g, conv_out_g,
           w_out, norm2_g, w_ffn_gate, w_ffn_up, conv_ffn_w, w_ffn_down):
    bsz, s, d = x.shape
    assert w_ada.shape[0] == 1, "single layer"
    assert s % TM_PROJ == 0 and s % TM_FFN == 0 and s % K_CHUNK == 0 and s % Q_TILE == 0
    assert s // SEL_BLOCK <= LANES and (s // SEL_BLOCK) % 8 == 0 and s >= WINDOW + Q_TILE
    d_ff = w_ffn_gate.shape[-1]
    assert d_ff % FF_CHUNK == 0
    n_ch = d_ff // FF_CHUNK
    ovl, gmat, qfeat, ktab, ctab = _static_tables(s)

    mod = _ada(c, w_ada[0], b_ada[0][None, :])
    mod3 = mod.reshape(bsz, 6, d)

    w = w_in[0]
    o_q, o_kv, o_gl = ATTN_WIDTH, ATTN_WIDTH + 6 * KV_COLS, ATTN_WIDTH + 6 * KV_COLS + N_GATE
    wq = w[:, :o_q].astype(BF16)
    wkv = w[:, o_q:o_kv].astype(BF16)
    perm = np.array([kv * GQA * 3 + h * 3 + br for br in range(3) for kv in range(N_KV) for h in range(GQA)])
    wgl = jnp.pad(w[:, o_kv:o_gl][:, perm], ((0, 0), (0, LANES - N_GATE))).astype(BF16)
    bg = jnp.pad(b_gate[0][perm], (0, LANES - N_GATE))[None, :]
    wc = w[:, o_gl:].astype(BF16)
    qg = jnp.tile(q_norm_g[0], GROUPS_PER_MXU_TILE)[None, :]
    kg = jnp.concatenate([jnp.tile(k_norm_slc_g[0], N_KV), jnp.tile(k_norm_win_g[0], N_KV)])[None, :]

    qt, kvc_raw, ks, vst, kw, vwt, gatet, conv_n = _in_proj(
        x, mod3, norm1_g, wq, wkv, wgl, wc, bg, qg, kg, conv_mix_w[0], conv_out_g, gmat, qfeat, ktab)

    half = CMP_STRIDE * HEAD_DIM
    pos2 = jnp.stack([pos_cmp_k[0].reshape(2, half), pos_cmp_v[0].reshape(2, half)])
    w1 = jnp.stack([w_cmp_k1[0], w_cmp_v1[0]]).reshape(2, 2, half, -1).astype(BF16)
    w2 = jnp.stack([w_cmp_k2[0], w_cmp_v2[0]]).astype(BF16)
    kc, vct = _compress(kvc_raw, pos2, w1, w2, k_norm_cmp_g, ctab)

    attn_n = _attention(qt, gatet, kc, vct, ks, vst, kw, vwt, ovl, attn_out_g[0].reshape(N_HEADS, HEAD_DIM).T)

    wo = w_out[0].astype(BF16)
    wg3 = w_ffn_gate[0].astype(BF16)
    wu3 = w_ffn_up[0].astype(BF16)
    wd = w_ffn_down[0].astype(BF16)
    cw3 = conv_ffn_w[0]
    return _out_ffn(x, attn_n, conv_n, mod3, norm2_g, wo, wg3, wu3, wd, cw3)
```

```python
import numpy as np
import jax
import jax.numpy as jnp
from jax import lax
from jax.experimental import pallas as pl
from jax.experimental.pallas import tpu as pltpu

F32 = jnp.float32
BF16 = jnp.bfloat16

N_HEADS = 8
N_KV = 2
GQA = N_HEADS // N_KV
HEAD_DIM = 64
ATTN_WIDTH = N_HEADS * HEAD_DIM
KV_COLS = N_KV * HEAD_DIM
N_GATE = 3 * N_HEADS
CMP_LEN = 32
CMP_STRIDE = 16
SEL_BLOCK = 64
SEL_TOPK = 16
WINDOW = 512
FORCE_SCORE = 1.0e4
CONV_WIDTH = 512
EPS = 1e-6
NEG = -1.0e30
TINY = float(np.finfo(np.float32).tiny)

LANES = 128
BF16_SUBLANES = 16
GROUPS_PER_MXU_TILE = 4
MXU_TILE = GROUPS_PER_MXU_TILE * HEAD_DIM
VMEM_LIMIT = 56 * 1024 * 1024

TM_PROJ = 1024
PROJ_SPLIT = 4
TM_FFN = 1024
FF_CHUNK = 256
Q_TILE = 128
K_CHUNK = 256
WIN_CHUNK = 128
HALO = 16

LOG2E = float(np.log2(np.e))
SLOPE_TERMS = 3
POS_FEATS = 2 * SLOPE_TERMS
GATE_ROWS = 32
V_ROWS = HEAD_DIM + BF16_SUBLANES
K_SEL_WIDTH = 2 * LANES


def _dot(a, b):
    return jnp.dot(a, b, preferred_element_type=F32)


def _group_rms(v, gmat, gain):
    ssq = _dot((v * v).astype(BF16), gmat)
    return v * lax.rsqrt(ssq * (1.0 / HEAD_DIM) + EPS) * gain


def _ada_kernel(c_ref, w_ref, b_ref, o_ref):
    a = jax.nn.silu(c_ref[...])
    o_ref[...] = jnp.dot(a, w_ref[...], preferred_element_type=F32,
                         precision=lax.Precision.HIGHEST) + b_ref[...]


def _ada(c, w, b):
    bsz, d = c.shape
    n = w.shape[1]
    return pl.pallas_call(
        _ada_kernel,
        out_shape=jax.ShapeDtypeStruct((bsz, n), F32),
        grid=(n // d,),
        in_specs=[pl.BlockSpec((bsz, d), lambda j: (0, 0)),
                  pl.BlockSpec((d, d), lambda j: (0, j)),
                  pl.BlockSpec((1, d), lambda j: (0, j))],
        out_specs=pl.BlockSpec((bsz, d), lambda j: (0, j)),
        compiler_params=pltpu.CompilerParams(dimension_semantics=("arbitrary",),
                                             vmem_limit_bytes=VMEM_LIMIT),
        name="ada",
    )(c, w, b)


def _inproj_kernel(x_ref, mod_ref, n1_ref, wq_ref, wkv_ref, wgl_ref, wc_ref, bg_ref, qg_ref, kg_ref,
                   cw_ref, cg_ref, gmat_ref, qfeat_ref, ktab_ref,
                   qt_out, kvc_out, ks_out, vst_out, kw_out, vwt_out, gatet_out, conv_out, u_scr):
    i = pl.program_id(1)
    tm = x_ref.shape[1]
    th = tm // PROJ_SPLIT
    mod = mod_ref[0]
    gmat = gmat_ref[...]
    cw = cw_ref[...]

    @pl.when(i == 0)
    def _():
        u_scr[0:HALO, :] = jnp.zeros((HALO, CONV_WIDTH), F32)

    proj = []
    for r0 in range(0, tm, th):
        x = x_ref[0, r0:r0 + th, :]
        ms = jnp.mean(x * x, axis=-1, keepdims=True)
        y = x * lax.rsqrt(ms + EPS) * n1_ref[...]
        hb = (y * (1.0 + mod[1:2]) + mod[0:1]).astype(BF16)
        proj.append((_dot(hb, wq_ref[...]), _dot(hb, wkv_ref[...]), _dot(hb, wgl_ref[...]), _dot(hb, wc_ref[...])))

    qfeat = jnp.broadcast_to(qfeat_ref[...], (th, ATTN_WIDTH))
    for (pq, pkv, pg, pc), r0 in zip(proj, range(0, tm, th)):
        rows = slice(r0, r0 + th)
        for j in range(ATTN_WIDTH // MXU_TILE):
            qn = _group_rms(pq[:, MXU_TILE * j:MXU_TILE * (j + 1)], gmat, qg_ref[...]) * (HEAD_DIM ** -0.5 * LOG2E)
            for hl in range(GROUPS_PER_MXU_TILE):
                hh = GROUPS_PER_MXU_TILE * j + hl
                qa = jnp.concatenate([qn[:, HEAD_DIM * hl:HEAD_DIM * (hl + 1)],
                                      qfeat[:, HEAD_DIM * hh:HEAD_DIM * (hh + 1)]], axis=-1)
                qt_out[0, hh, :, rows] = qa.T.astype(BF16)

        for j in range(2 * N_KV):
            kvc_out[0, j, rows, :] = pkv[:, HEAD_DIM * j:HEAD_DIM * (j + 1)]
        ksw = jnp.concatenate([pkv[:, 2 * KV_COLS:3 * KV_COLS], pkv[:, 4 * KV_COLS:5 * KV_COLS]], axis=-1)
        kswn = _group_rms(ksw, gmat, kg_ref[...])
        ktab = ktab_ref[rows, :].astype(F32)
        vs_t = pkv[:, 3 * KV_COLS:4 * KV_COLS].T.astype(BF16)
        vw_t = pkv[:, 5 * KV_COLS:6 * KV_COLS].T.astype(BF16)
        for g in range(N_KV):
            ks_out[0, g, rows, :] = jnp.concatenate(
                [kswn[:, HEAD_DIM * g:HEAD_DIM * (g + 1)], ktab], axis=-1).astype(BF16)
            kw_out[0, g, rows, :] = jnp.concatenate(
                [kswn[:, KV_COLS + HEAD_DIM * g:KV_COLS + HEAD_DIM * (g + 1)], ktab[:, 0:HEAD_DIM]],
                axis=-1).astype(BF16)
            for out, v_t, kc_n in ((vst_out, vs_t, K_CHUNK), (vwt_out, vw_t, WIN_CHUNK)):
                ones_rows = jnp.where(lax.broadcasted_iota(jnp.int32, (BF16_SUBLANES, kc_n), 0) == 0,
                                      1.0, 0.0).astype(BF16)
                for j in range(th // kc_n):
                    cj = r0 // kc_n + j
                    out[0, g, cj, 0:HEAD_DIM, :] = v_t[HEAD_DIM * g:HEAD_DIM * (g + 1), kc_n * j:kc_n * (j + 1)]
                    out[0, g, cj, HEAD_DIM:V_ROWS, :] = ones_rows

        gate = jax.nn.sigmoid(pg + bg_ref[...])
        gatet_out[0, :, rows] = gate.T[0:GATE_ROWS]

        gate_b = pc[:, 0:CONV_WIDTH]
        u = pc[:, CONV_WIDTH:2 * CONV_WIDTH] * pc[:, 2 * CONV_WIDTH:3 * CONV_WIDTH]
        u_scr[HALO + r0:HALO + r0 + th, :] = u
        conv = gate_b * (cw[0:1] * u_scr[pl.ds(HALO + r0 - 2, th), :] + cw[1:2] * u_scr[pl.ds(HALO + r0 - 1, th), :]
                         + cw[2:3] * u)
        for j in range(CONV_WIDTH // MXU_TILE):
            sl = slice(MXU_TILE * j, MXU_TILE * (j + 1))
            conv_out[0, rows, sl] = _group_rms(conv[:, sl], gmat, cg_ref[:, sl]).astype(BF16)
    u_scr[0:HALO, :] = u_scr[tm:tm + HALO, :]


def _in_proj(x, mod3, n1, wq, wkv, wgl, wc, bg, qg, kg, cw, cg, gmat, qfeat, ktab):
    bsz, s, d = x.shape
    tm = TM_PROJ
    const = lambda shape: pl.BlockSpec(shape, lambda b, i: (0,) * len(shape))
    tok = lambda w: pl.BlockSpec((1, tm, w), lambda b, i: (b, i, 0))
    tab = lambda w: pl.BlockSpec((tm, w), lambda b, i: (i, 0))
    kspec = lambda w: pl.BlockSpec((1, N_KV, tm, w), lambda b, i: (b, 0, i, 0))
    vtspec = lambda kc: pl.BlockSpec((1, N_KV, tm // kc, V_ROWS, kc), lambda b, i: (b, 0, i, 0, 0))
    vt_shape = lambda kc: jax.ShapeDtypeStruct((bsz, N_KV, s // kc, V_ROWS, kc), BF16)
    return pl.pallas_call(
        _inproj_kernel,
        out_shape=(jax.ShapeDtypeStruct((bsz, N_HEADS, LANES, s), BF16),
                   jax.ShapeDtypeStruct((bsz, 2 * N_KV, s, HEAD_DIM), F32),
                   jax.ShapeDtypeStruct((bsz, N_KV, s, K_SEL_WIDTH), BF16), vt_shape(K_CHUNK),
                   jax.ShapeDtypeStruct((bsz, N_KV, s, LANES), BF16), vt_shape(WIN_CHUNK),
                   jax.ShapeDtypeStruct((bsz, GATE_ROWS, s), F32),
                   jax.ShapeDtypeStruct((bsz, s, CONV_WIDTH), BF16)),
        grid=(bsz, s // tm),
        in_specs=[tok(d),
                  pl.BlockSpec((1, 6, d), lambda b, i: (b, 0, 0)),
                  const(n1.shape), const(wq.shape), const(wkv.shape), const(wgl.shape), const(wc.shape),
                  const(bg.shape), const(qg.shape), const(kg.shape), const(cw.shape), const(cg.shape),
                  const(gmat.shape), const(qfeat.shape), tab(ktab.shape[1])],
        out_specs=(pl.BlockSpec((1, N_HEADS, LANES, tm), lambda b, i: (b, 0, 0, i)),
                   pl.BlockSpec((1, 2 * N_KV, tm, HEAD_DIM), lambda b, i: (b, 0, i, 0)),
                   kspec(K_SEL_WIDTH), vtspec(K_CHUNK), kspec(LANES), vtspec(WIN_CHUNK),
                   pl.BlockSpec((1, GATE_ROWS, tm), lambda b, i: (b, 0, i)),
                   tok(CONV_WIDTH)),
        scratch_shapes=[pltpu.VMEM((tm + HALO, CONV_WIDTH), F32)],
        compiler_params=pltpu.CompilerParams(dimension_semantics=("arbitrary", "arbitrary"),
                                             vmem_limit_bytes=VMEM_LIMIT),
        name="in_proj",
    )(x, mod3, n1, wq, wkv, wgl, wc, bg, qg, kg, cw, cg, gmat, qfeat, ktab)


def _cmp_kernel(kraw_ref, vraw_ref, pos_ref, w1_ref, w2_ref, g_ref, ctab_ref, kc_out, vct_out):
    ncp = kraw_ref.shape[2] // CMP_STRIDE

    def mlp(j):
        raw_ref = (kraw_ref, vraw_ref)[j]
        a = jnp.concatenate([raw_ref[0, 0, pl.ds(l, ncp, stride=CMP_STRIDE), :] for l in range(CMP_STRIDE)],
                            axis=-1)
        pos = pos_ref[j]
        ha = _dot((a + pos[0:1]).astype(BF16), w1_ref[j, 0])
        hb = _dot((a + pos[1:2]).astype(BF16), w1_ref[j, 1])
        h = ha + pltpu.roll(hb, ncp - 1, axis=0)
        return _dot(jax.nn.gelu(h).astype(BF16), w2_ref[j])

    ck = mlp(0)
    ckn = ck * lax.rsqrt(jnp.mean(ck * ck, axis=-1, keepdims=True) + EPS) * g_ref[...]
    kc_out[0, 0] = jnp.concatenate([ckn.astype(BF16), ctab_ref[...]], axis=-1)
    cv = mlp(1)
    cv_t = jnp.concatenate([cv, jnp.zeros((ncp, LANES - HEAD_DIM), F32)], axis=-1).T
    vct_out[0, 0, 0:HEAD_DIM, :] = cv_t[0:HEAD_DIM].astype(BF16)
    vct_out[0, 0, HEAD_DIM:V_ROWS, :] = jnp.where(
        lax.broadcasted_iota(jnp.int32, (BF16_SUBLANES, ncp), 0) == 0, 1.0, 0.0).astype(BF16)


def _compress(kvc_raw, pos2, w1, w2, g, ctab):
    bsz, _, s, _ = kvc_raw.shape
    ncp = s // CMP_STRIDE
    full = lambda a: pl.BlockSpec(a.shape, lambda b, j: (0,) * a.ndim)
    return pl.pallas_call(
        _cmp_kernel,
        out_shape=(jax.ShapeDtypeStruct((bsz, N_KV, ncp, LANES), BF16),
                   jax.ShapeDtypeStruct((bsz, N_KV, V_ROWS, ncp), BF16)),
        grid=(bsz, N_KV),
        in_specs=[pl.BlockSpec((1, 1, s, HEAD_DIM), lambda b, j: (b, j, 0, 0)),
                  pl.BlockSpec((1, 1, s, HEAD_DIM), lambda b, j: (b, N_KV + j, 0, 0)),
                  full(pos2), full(w1), full(w2), full(g), full(ctab)],
        out_specs=(pl.BlockSpec((1, 1, ncp, LANES), lambda b, j: (b, j, 0, 0)),
                   pl.BlockSpec((1, 1, V_ROWS, ncp), lambda b, j: (b, j, 0, 0))),
        compiler_params=pltpu.CompilerParams(dimension_semantics=("arbitrary", "arbitrary"),
                                             vmem_limit_bytes=VMEM_LIMIT),
        name="compress",
    )(kvc_raw, kvc_raw, pos2, w1, w2, g, ctab)


SOFTMAX_SLAB = 64


def _softmax_step(state, s, vt, mask, n_rep):
    m, acc = state
    kc = s.shape[0]
    if mask is not None:
        dist, hi = mask
        t = dist.shape[1]
        keep = {r: (dist[r:r + SOFTMAX_SLAB] >= 0) & (dist[r:r + SOFTMAX_SLAB] < hi)
                for r in range(0, kc, SOFTMAX_SLAB)}

    def slab(r):
        blk = s[r:r + SOFTMAX_SLAB, :]
        if mask is None:
            return blk
        return jnp.concatenate([jnp.where(keep[r], blk[:, t * h:t * (h + 1)], NEG) for h in range(n_rep)], axis=1)

    m_new = m
    for r in range(0, kc, SOFTMAX_SLAB):
        m_new = jnp.maximum(m_new, jnp.max(slab(r), axis=0, keepdims=True))
    p = jnp.concatenate([jnp.exp2(slab(r) - m_new).astype(BF16) for r in range(0, kc, SOFTMAX_SLAB)], axis=0)
    alpha = jnp.exp2(m - m_new)
    return m_new, alpha * acc + _dot(vt, p)


def _attn_kernel(qt_ref, gt_ref, kc_ref, vct_ref, ks_ref, vst_ref, kw_ref, vwt_ref, ovl_ref, ogt_ref,
                 o_ref, score_scr, qsel_scr, s_scr):
    T = qt_ref.shape[3]
    KC = K_CHUNK
    NB, NCP = ovl_ref.shape
    nsel = min(SEL_TOPK, NB)
    i = pl.program_id(1)
    t0 = i * T
    diag = (t0 + T - 1) // KC

    tcol = lax.broadcasted_iota(jnp.int32, (KC, T), 1)
    krow = lax.broadcasted_iota(jnp.int32, (KC, T), 0)
    rel = tcol - krow

    cend = CMP_STRIDE * lax.broadcasted_iota(jnp.int32, (NCP, T), 0) + (CMP_LEN - 1)
    mask_c = cend <= t0 + lax.broadcasted_iota(jnp.int32, (NCP, T), 1)

    blk = lax.broadcasted_iota(jnp.int32, (NB, T), 0)
    cur = (t0 + lax.broadcasted_iota(jnp.int32, (NB, T), 1)) // SEL_BLOCK
    valid = blk <= cur
    forced = (blk == 0) | (blk == cur) | (blk == cur - 1)
    row8 = lax.broadcasted_iota(jnp.int32, (8, T), 0)

    win_keys = WINDOW + T
    w0 = pl.multiple_of(jnp.maximum(t0 - WINDOW, 0), WIN_CHUNK)
    win_mask = ((t0 - w0) + lax.broadcasted_iota(jnp.int32, (win_keys, T), 1)
                - lax.broadcasted_iota(jnp.int32, (win_keys, T), 0), WINDOW)

    def init_state():
        return (jnp.full((1, GQA * T), 2.0 * NEG, F32), jnp.zeros((V_ROWS, GQA * T), F32))

    def key_rows(k_ref, g, c):
        return k_ref[0, g, pl.ds(pl.multiple_of(c * KC, KC), KC), :]

    qt4 = [jnp.concatenate([qt_ref[0, GQA * g + h] for h in range(GQA)], axis=1) for g in range(N_KV)]
    oc_t, q_sel = [None] * N_KV, [None] * N_KV
    win_state = [init_state() for _ in range(N_KV)]

    def finish_compressed(g, s):
        ps = []
        for h in range(GQA):
            sh = jnp.where(mask_c, s[:, T * h:T * (h + 1)], NEG)
            m = jnp.max(sh, axis=0, keepdims=True)
            m = jnp.where(m > 0.5 * NEG, m, 0.0)
            e = jnp.exp2(sh - m)
            ps.append(e / jnp.maximum(jnp.sum(e, axis=0, keepdims=True), TINY))
        oc_t[g] = _dot(vct_ref[0, g], jnp.concatenate(ps, axis=1).astype(BF16))

        psum = ((ps[0] + ps[1]) + ps[2]) + ps[3]
        p_hi = psum.astype(BF16)
        p_lo = (psum - p_hi.astype(F32)).astype(BF16)
        ovl = ovl_ref[...]
        imp_t = _dot(ovl, p_hi) + _dot(ovl, p_lo)
        score = jnp.where(forced, FORCE_SCORE, jnp.where(valid, imp_t, -1.0))
        score_scr[g] = score
        n_grp = NB // 8
        grp = [score[8 * r:8 * (r + 1)] for r in range(n_grp)]
        rank = [jnp.zeros((8, T), jnp.int32) for _ in range(n_grp)]
        for ii in range(NB):
            si = score_scr[g, pl.ds(ii, 1), :]
            r0 = ii // 8
            for r in range(n_grp):
                ge = jnp.where(si >= grp[r], 1, 0)
                gt = jnp.where(si > grp[r], 1, 0)
                if r > r0:
                    beats = ge
                elif r < r0:
                    beats = gt
                else:
                    beats = jnp.where(row8 + 8 * r > ii, ge, gt)
                rank[r] = rank[r] + beats
        sel = (jnp.concatenate(rank, axis=0) < nsel) & valid
        selb = jnp.where(sel, 0.0, NEG)
        if NB < LANES:
            selb = jnp.concatenate([selb, jnp.full((LANES - NB, T), NEG, F32)], axis=0)
        selb = selb.astype(BF16)
        q_sel[g] = jnp.concatenate([qt4[g], jnp.concatenate([selb] * GQA, axis=1)], axis=0)

    def finish_window(g, s):
        vt = jnp.concatenate([vwt_ref[0, g, w0 // WIN_CHUNK + j] for j in range(win_keys // WIN_CHUNK)], axis=1)
        win_state[g] = _softmax_step(win_state[g], s, vt, win_mask, GQA)

    units = [("cmp", g) for g in range(N_KV)] + [("win", g) for g in range(N_KV)]
    lookahead = 2
    scores = []
    for u, (kind, g) in enumerate(units):
        while len(scores) < min(len(units), u + 1 + lookahead):
            kind_n, g_n = units[len(scores)]
            keys = kc_ref[0, g_n] if kind_n == "cmp" else kw_ref[0, g_n, pl.ds(w0, win_keys), :]
            scores.append(_dot(keys, qt4[g_n]))
        if kind == "cmp":
            finish_compressed(g, scores[u])
        else:
            finish_window(g, scores[u])

    for g in range(N_KV):
        qsel_scr[g] = q_sel[g]

    def sel_scores(slot, g, c):
        s_scr[slot, g] = _dot(key_rows(ks_ref, g, c), qsel_scr[g])

    last_chunk = ks_ref.shape[2] // KC - 1

    def sel_body(j, states):
        states = list(states)
        for slot in range(2):
            c = 2 * j + slot
            for g in range(N_KV):
                sel_scores(1 - slot, g, c + 1)
            for g in range(N_KV):
                states[g] = _softmax_step(states[g], s_scr.at[slot, g], vst_ref[0, g, c], None, GQA)
        return tuple(states)

    for g in range(N_KV):
        sel_scores(0, g, 0)
    n_pairs = diag // 2
    n_quads = n_pairs // 2
    sel_state = lax.fori_loop(0, n_quads, lambda q, st: sel_body(2 * q + 1, sel_body(2 * q, st)),
                              tuple(init_state() for _ in range(N_KV)))
    sel_state = list(lax.fori_loop(2 * n_quads, n_pairs, sel_body, sel_state))
    def tail_step(states, slot):
        c = 2 * n_pairs + slot
        c_load = jnp.minimum(c, last_chunk)
        causal = (rel + (t0 - c * KC), 1 << 30)
        return tuple(_softmax_step(states[g], s_scr.at[slot, g], vst_ref[0, g, c_load], causal, GQA)
                     for g in range(N_KV))

    for g in range(N_KV):
        sel_scores(1, g, jnp.minimum(2 * n_pairs + 1, last_chunk))
    sel_state = tail_step(tuple(sel_state), 0)
    sel_state = lax.cond(2 * n_pairs + 1 <= diag, lambda st: tail_step(st, 1), lambda st: st, sel_state)
    os_t, ow_t = [], []
    for g in range(N_KV):
        acc = sel_state[g][1]
        os_t.append(acc[0:HEAD_DIM] / jnp.maximum(acc[HEAD_DIM:HEAD_DIM + 1], TINY))
        acc = win_state[g][1]
        ow_t.append(acc[0:HEAD_DIM] / jnp.maximum(acc[HEAD_DIM:HEAD_DIM + 1], TINY))

    gt = gt_ref[0]
    ogt = ogt_ref[...]
    pieces = []
    for hh in range(N_HEADS):
        g, h = divmod(hh, GQA)
        sl = slice(T * h, T * (h + 1))
        o = (gt[hh:hh + 1] * oc_t[g][0:HEAD_DIM, sl] + gt[N_HEADS + hh:N_HEADS + hh + 1] * os_t[g][:, sl]
             + gt[2 * N_HEADS + hh:2 * N_HEADS + hh + 1] * ow_t[g][:, sl])
        msq = jnp.mean(o * o, axis=0, keepdims=True)
        pieces.append(o * lax.rsqrt(msq + EPS) * ogt[:, hh:hh + 1])
    o_ref[0] = jnp.concatenate(pieces, axis=0).T.astype(BF16)


def _attention(qt, gatet, kc, vct, ks, vst, kw, vwt, ovl, ogt):
    bsz, _, _, s = qt.shape
    T = Q_TILE
    ncp = kc.shape[2]
    nb = ovl.shape[0]
    full4 = lambda a: pl.BlockSpec((1,) + a.shape[1:], lambda b, i: (b,) + (0,) * (a.ndim - 1))
    return pl.pallas_call(
        _attn_kernel,
        out_shape=jax.ShapeDtypeStruct((bsz, s, ATTN_WIDTH), BF16),
        grid=(bsz, s // T),
        in_specs=[pl.BlockSpec((1, N_HEADS, LANES, T), lambda b, i: (b, 0, 0, i)),
                  pl.BlockSpec((1, GATE_ROWS, T), lambda b, i: (b, 0, i)),
                  full4(kc), full4(vct), full4(ks), full4(vst), full4(kw), full4(vwt),
                  pl.BlockSpec(ovl.shape, lambda b, i: (0, 0)),
                  pl.BlockSpec(ogt.shape, lambda b, i: (0, 0))],
        out_specs=pl.BlockSpec((1, T, ATTN_WIDTH), lambda b, i: (b, i, 0)),
        scratch_shapes=[pltpu.VMEM((N_KV, nb, T), F32),
                        pltpu.VMEM((N_KV, K_SEL_WIDTH, GQA * T), BF16),
                        pltpu.VMEM((2, N_KV, K_CHUNK, GQA * T), F32)],
        compiler_params=pltpu.CompilerParams(dimension_semantics=("arbitrary", "arbitrary"),
                                             vmem_limit_bytes=VMEM_LIMIT),
        name="nsa_attn",
    )(qt, gatet, kc, vct, ks, vst, kw, vwt, ovl, ogt)


def _ffn_kernel(x_ref, a_ref, cv_ref, mod_ref, n2_ref, wo_ref, wg_ref, wu_ref, wd_ref, cw_ref,
                o_ref, h2_scr, gp_scr, act_scr):
    i = pl.program_id(1)
    tm = x_ref.shape[1]
    n_chunks = wg_ref.shape[1] // FF_CHUNK
    mod = mod_ref[0]
    mix = _dot(a_ref[0], wo_ref[0:ATTN_WIDTH, :]) + _dot(cv_ref[0], wo_ref[ATTN_WIDTH:ATTN_WIDTH + CONV_WIDTH, :])
    x1 = x_ref[0] + mod[2:3] * mix
    o_ref[0] = x1
    ms = jnp.mean(x1 * x1, axis=-1, keepdims=True)
    h2 = (x1 * lax.rsqrt(ms + EPS) * n2_ref[...]) * (1.0 + mod[4:5]) + mod[3:4]

    @pl.when(i == 0)
    def _():
        h2_scr[0:HALO, :] = jnp.zeros((HALO, h2_scr.shape[1]), BF16)

    h2_scr[HALO:HALO + tm, :] = h2.astype(BF16)

    for c in range(n_chunks):
        slot = c % 2
        cols = slice(FF_CHUNK * c, FF_CHUNK * (c + 1))
        gp_scr[slot] = _dot(h2_scr[...], wg_ref[:, cols])
        cw = cw_ref[:, cols]
        g_pre = (cw[0:1] * gp_scr[slot, pl.ds(HALO - 2, tm), :] + cw[1:2] * gp_scr[slot, pl.ds(HALO - 1, tm), :]
                 + cw[2:3] * gp_scr[slot, pl.ds(HALO, tm), :])
        up = _dot(h2_scr[pl.ds(HALO, tm), :], wu_ref[:, cols])
        act_scr[:, cols] = (jax.nn.silu(g_pre) * up).astype(BF16)

    o_ref[0] = o_ref[0] + mod[5:6] * _dot(act_scr[...], wd_ref[...])
    h2_scr[0:HALO, :] = h2_scr[tm:tm + HALO, :]


def _out_ffn(x, attn_n, conv_n, mod3, n2, wo, wg3, wu3, wd, cw3):
    bsz, s, d = x.shape
    tm = TM_FFN
    resident = lambda shape: pl.BlockSpec(shape, lambda b, i: (0,) * len(shape), pipeline_mode=pl.Buffered(1))
    tok = lambda w: pl.BlockSpec((1, tm, w), lambda b, i: (b, i, 0))
    return pl.pallas_call(
        _ffn_kernel,
        out_shape=jax.ShapeDtypeStruct((bsz, s, d), F32),
        grid=(bsz, s // tm),
        in_specs=[tok(d), tok(ATTN_WIDTH), tok(CONV_WIDTH),
                  pl.BlockSpec((1, 6, d), lambda b, i: (b, 0, 0)),
                  resident(n2.shape), resident(wo.shape), resident(wg3.shape), resident(wu3.shape),
                  resident(wd.shape), resident(cw3.shape)],
        out_specs=tok(d),
        scratch_shapes=[pltpu.VMEM((tm + HALO, d), BF16),
                        pltpu.VMEM((2, tm + HALO, FF_CHUNK), F32),
                        pltpu.VMEM((tm, wd.shape[0]), BF16)],
        compiler_params=pltpu.CompilerParams(dimension_semantics=("arbitrary", "arbitrary"),
                                             vmem_limit_bytes=VMEM_LIMIT),
        name="out_ffn",
    )(x, attn_n, conv_n, mod3, n2, wo, wg3, wu3, wd, cw3)


def _pos_feats(pos):
    return np.stack([pos // SEL_BLOCK, pos % SEL_BLOCK] * SLOPE_TERMS, axis=-1)


def _bf16_round(v):
    bits = np.asarray(v, np.float32).view(np.uint32)
    bits = (bits + np.uint32(0x7FFF) + ((bits >> np.uint32(16)) & np.uint32(1))) & np.uint32(0xFFFF0000)
    return bits.view(np.float32)


def _static_tables(s):
    ncp = s // CMP_STRIDE
    n_cmp = (s - CMP_LEN) // CMP_STRIDE + 1
    nb = s // SEL_BLOCK
    cs = np.arange(ncp) * CMP_STRIDE
    bs = np.arange(nb) * SEL_BLOCK
    ovl = ((cs[None, :] < bs[:, None] + SEL_BLOCK) & (cs[None, :] + CMP_LEN > bs[:, None])
           & (np.arange(ncp)[None, :] < n_cmp)).astype(np.float32)
    gmat = (np.arange(MXU_TILE)[:, None] // HEAD_DIM == np.arange(MXU_TILE)[None, :] // HEAD_DIM)

    t = np.arange(s)
    qfeat = np.zeros((N_HEADS, HEAD_DIM), np.float32)
    for hh in range(N_HEADS):
        rest = np.float32(2.0 ** (-8.0 * (hh + 1) / N_HEADS) * LOG2E)
        for i in range(SLOPE_TERMS):
            term = _bf16_round(rest)
            rest = np.float32(rest - term)
            qfeat[hh, 2 * i] = term * SEL_BLOCK
            qfeat[hh, 2 * i + 1] = term
    ktab = np.zeros((s, K_SEL_WIDTH - HEAD_DIM), np.float32)
    ktab[:, 0:POS_FEATS] = _pos_feats(t)
    ktab[t, HEAD_DIM + t // SEL_BLOCK] = 1.0
    ctab = np.zeros((ncp, LANES - HEAD_DIM), np.float32)
    ctab[:, 0:POS_FEATS] = _pos_feats(cs + CMP_LEN - 1)
    as_bf16 = lambda a: jnp.asarray(a, BF16)
    return (as_bf16(ovl), as_bf16(gmat.astype(np.float32)), jnp.asarray(qfeat.reshape(1, ATTN_WIDTH)),
            as_bf16(ktab), as_bf16(ctab))


def kernel(x, c, w_ada, b_ada, norm1_g, w_in, b_gate, q_norm_g, k_norm_cmp_g, k_norm_slc_g, k_norm_win_g,
           pos_cmp_k, pos_cmp_v, w_cmp_k1, w_cmp_k2, w_cmp_v1, w_cmp_v2, conv_mix_w, attn_out_g, conv_out_g,
           w_out, norm2_g, w_ffn_gate, w_ffn_up, conv_ffn_w, w_ffn_down):
    bsz, s, d = x.shape
    assert w_ada.shape[0] == 1, "single layer"
    assert s % TM_PROJ == 0 and s % TM_FFN == 0 and s % K_CHUNK == 0 and s % Q_TILE == 0
    assert s // SEL_BLOCK <= LANES and (s // SEL_BLOCK) % 8 == 0 and s >= WINDOW + Q_TILE
    d_ff = w_ffn_gate.shape[-1]
    assert d_ff % FF_CHUNK == 0
    n_ch = d_ff // FF_CHUNK
    ovl, gmat, qfeat, ktab, ctab = _static_tables(s)

    mod = _ada(c, w_ada[0], b_ada[0][None, :])
    mod3 = mod.reshape(bsz, 6, d)

    w = w_in[0]
    o_q, o_kv, o_gl = ATTN_WIDTH, ATTN_WIDTH + 6 * KV_COLS, ATTN_WIDTH + 6 * KV_COLS + N_GATE
    wq = w[:, :o_q].astype(BF16)
    wkv = w[:, o_q:o_kv].astype(BF16)
    perm = np.array([kv * GQA * 3 + h * 3 + br for br in range(3) for kv in range(N_KV) for h in range(GQA)])
    wgl = jnp.pad(w[:, o_kv:o_gl][:, perm], ((0, 0), (0, LANES - N_GATE))).astype(BF16)
    bg = jnp.pad(b_gate[0][perm], (0, LANES - N_GATE))[None, :]
    wc = w[:, o_gl:].astype(BF16)
    qg = jnp.tile(q_norm_g[0], GROUPS_PER_MXU_TILE)[None, :]
    kg = jnp.concatenate([jnp.tile(k_norm_slc_g[0], N_KV), jnp.tile(k_norm_win_g[0], N_KV)])[None, :]

    qt, kvc_raw, ks, vst, kw, vwt, gatet, conv_n = _in_proj(
        x, mod3, norm1_g, wq, wkv, wgl, wc, bg, qg, kg, conv_mix_w[0], conv_out_g, gmat, qfeat, ktab)

    half = CMP_STRIDE * HEAD_DIM
    pos2 = jnp.stack([pos_cmp_k[0].reshape(2, half), pos_cmp_v[0].reshape(2, half)])
    w1 = jnp.stack([w_cmp_k1[0], w_cmp_v1[0]]).reshape(2, 2, half, -1).astype(BF16)
    w2 = jnp.stack([w_cmp_k2[0], w_cmp_v2[0]]).astype(BF16)
    kc, vct = _compress(kvc_raw, pos2, w1, w2, k_norm_cmp_g, ctab)

    attn_n = _attention(qt, gatet, kc, vct, ks, vst, kw, vwt, ovl, attn_out_g[0].reshape(N_HEADS, HEAD_DIM).T)

    wo = w_out[0].astype(BF16)
    wg3 = w_ffn_gate[0].astype(BF16)
    wu3 = w_ffn_up[0].astype(BF16)
    wd = w_ffn_down[0].astype(BF16)
    cw3 = conv_ffn_w[0]
    return _out_ffn(x, attn_n, conv_n, mod3, norm2_g, wo, wg3, wu3, wd, cw3)
```
